```python
import jax, jax.numpy as jnp
from jax import lax
import numpy as np

D_MODEL = 1024
BATCH = 4
SEQ = 4096
DEPTH = 4
DEC_BATCH = 128
DEC_SEQ = 8
PAST_LEN = 2048
PAGE_SIZE = 128

N_MIX_LAYERS = (DEPTH + 1) // 2
N_SB_LAYERS = DEPTH // 2
POOL_WIDTH = D_MODEL // 2
POOL_WINDOWS = (2, 4, 8, 16)
N_POOL_GROUPS = len(POOL_WINDOWS)
POOL_GROUP = POOL_WIDTH // N_POOL_GROUPS
POOL_BUF = max(POOL_WINDOWS) - 1
CONV_WIDTH = D_MODEL // 2
CONV_TAPS = 31
CONV_BUF = CONV_TAPS - 1
MIX_IN = POOL_WIDTH + 2 * CONV_WIDTH
MIX_OUT = POOL_WIDTH + CONV_WIDTH
N_HEADS = 16
HEAD_DIM = D_MODEL // N_HEADS
D_FF = 4 * D_MODEL
Q_BLOCK = 128
SB_BIAS_INIT = -8.0
EPS = 1e-6

kernel_name = "hybrid_pool_conformer_stickbreaking_decoder_step"


def rmsnorm(x, g):
    xf = x.astype(jnp.float32)
    y = xf * lax.rsqrt(jnp.mean(xf * xf, axis=-1, keepdims=True) + EPS)
    return (y * g.astype(jnp.float32)).astype(x.dtype)


def layernorm(x, g, b):
    xf = x.astype(jnp.float32)
    mu = jnp.mean(xf, axis=-1, keepdims=True)
    var = jnp.mean(jnp.square(xf - mu), axis=-1, keepdims=True)
    y = (xf - mu) * lax.rsqrt(var + EPS)
    return (y * g.astype(jnp.float32) + b.astype(jnp.float32)).astype(x.dtype)


def pool_mix(u_full, n_hist, pos0, w_grp, scale):
    B, L, C = u_full.shape
    T = L - n_hist
    uf = u_full.astype(jnp.float32)
    csum = jnp.concatenate([jnp.zeros((B, 1, C), jnp.float32), jnp.cumsum(uf, axis=1)], axis=1)
    idx = n_hist + jnp.arange(T)
    pos = pos0 + jnp.arange(T)
    hi = csum[:, idx + 1]
    u_new = uf[:, n_hist:]
    diffs = []
    for g, w in enumerate(POOL_WINDOWS):
        sl = slice(g * POOL_GROUP, (g + 1) * POOL_GROUP)
        cnt = jnp.minimum(w, pos + 1)
        lo = csum[:, idx + 1 - cnt, sl]
        mean = (hi[..., sl] - lo) / cnt[None, :, None].astype(jnp.float32)
        diffs.append(mean - u_new[..., sl])
    d = jnp.stack(diffs, axis=2)
    y = jnp.einsum('btgc,gcd->btgd', d, w_grp.astype(jnp.float32)).reshape(B, T, C)
    return (y * scale.astype(jnp.float32)).astype(u_full.dtype)


def conv_module(g_full, conv_w, conv_b, ln_g, ln_b):
    y = lax.conv_general_dilated(g_full, conv_w[:, None, :], window_strides=(1,), padding='VALID',
                                 dimension_numbers=('NWC', 'WIO', 'NWC'),
                                 feature_group_count=CONV_WIDTH)
    y = layernorm(y + conv_b, ln_g, ln_b)
    return jax.nn.silu(y)


def mix_layer(h, pool_hist, conv_hist, pos0, norm_g, w_in, w_out, w_grp, scale,
              conv_w, conv_b, ln_g, ln_b):
    xn = rmsnorm(h, norm_g)
    proj = xn @ w_in
    u = proj[..., :POOL_WIDTH]
    a = proj[..., POOL_WIDTH:POOL_WIDTH + CONV_WIDTH]
    gate = proj[..., POOL_WIDTH + CONV_WIDTH:]
    glu = a * jax.nn.sigmoid(gate)
    u_full = jnp.concatenate([pool_hist, u], axis=1)
    g_full = jnp.concatenate([conv_hist, glu], axis=1)
    p_out = pool_mix(u_full, pool_hist.shape[1], pos0, w_grp, scale)
    c_out = conv_module(g_full, conv_w, conv_b, ln_g, ln_b)
    out = jnp.concatenate([p_out, c_out], axis=-1) @ w_out
    return h + out, u_full[:, -POOL_BUF:], g_full[:, -CONV_BUF:]


def sb_attend(q, k, v, q_pos, k_pos, bias):
    z = jnp.einsum('bqhd,bkhd->bhqk', q.astype(jnp.float32), k.astype(jnp.float32)) * (HEAD_DIM ** -0.5)
    z = z + bias.astype(jnp.float32)[None, :, None, None]
    mask = k_pos[None, :] < q_pos[:, None]
    sp = jnp.where(mask, jax.nn.softplus(z), 0.0)
    suffix = lax.cumsum(sp, axis=3, reverse=True) - sp
    a = jnp.where(mask, jnp.exp(jax.nn.log_sigmoid(z) - suffix), 0.0)
    o = jnp.einsum('bhqk,bkhd->bqhd', a, v.astype(jnp.float32))
    return o.astype(q.dtype)


def sb_qkv(h, norm_g, w_qkv):
    B, T, _ = h.shape
    qkv = (rmsnorm(h, norm_g) @ w_qkv).reshape(B, T, 3, N_HEADS, HEAD_DIM)
    return qkv[:, :, 0], qkv[:, :, 1], qkv[:, :, 2]


def sb_prompt(q, k, v, bias):
    B, S, H, Dh = q.shape
    nb = S // Q_BLOCK
    qb = q.reshape(B, nb, Q_BLOCK, H, Dh).transpose(1, 0, 2, 3, 4)
    k_pos = jnp.arange(S)

    def blk(args):
        i, q_i = args
        q_pos = i * Q_BLOCK + jnp.arange(Q_BLOCK)
        return sb_attend(q_i, k, v, q_pos, k_pos, bias)

    o = lax.map(blk, (jnp.arange(nb), qb))
    return o.transpose(1, 0, 2, 3, 4).reshape(B, S, H, Dh)


def mlp(h, norm_g, w_up, w_down):
    xn = rmsnorm(h, norm_g)
    return h + jnp.square(jax.nn.relu(xn @ w_up)) @ w_down


def setup_inputs(seed: int = 0) -> dict:
    key = jax.random.key(seed)
    ks = jax.random.split(key, 24)
    f32 = jnp.float32
    n_pages = PAST_LEN // PAGE_SIZE
    n_used = DEC_BATCH * n_pages
    n_pool_pages = n_used + n_used // 4
    nrm = lambda k, shape, s: jax.random.normal(k, shape, f32) * s
    page_table = jax.random.permutation(ks[0], n_pool_pages)[:n_used].reshape(DEC_BATCH, n_pages).astype(jnp.int32)
    return {
        "x_prompt": nrm(ks[1], (BATCH, SEQ, D_MODEL), 1.0),
        "x_sample": nrm(ks[2], (DEC_BATCH, DEC_SEQ, D_MODEL), 1.0),
        "cache_k": nrm(ks[3], (N_SB_LAYERS, n_pool_pages, PAGE_SIZE, N_HEADS, HEAD_DIM), 1.0),
        "cache_v": nrm(ks[4], (N_SB_LAYERS, n_pool_pages, PAGE_SIZE, N_HEADS, HEAD_DIM), 1.0),
        "state_pool": nrm(ks[5], (N_MIX_LAYERS, DEC_BATCH, POOL_BUF, POOL_WIDTH), 1.0),
        "state_conv": nrm(ks[6], (N_MIX_LAYERS, DEC_BATCH, CONV_BUF, CONV_WIDTH), 1.0),
        "page_table": page_table,
        "norm_mix": 1.0 + nrm(ks[7], (DEPTH, D_MODEL), 0.05),
        "norm_mlp": 1.0 + nrm(ks[8], (DEPTH, D_MODEL), 0.05),
        "norm_final": 1.0 + nrm(ks[9], (D_MODEL,), 0.05),
        "w_in_mix": nrm(ks[10], (N_MIX_LAYERS, D_MODEL, MIX_IN), D_MODEL ** -0.5),
        "w_out_mix": nrm(ks[11], (N_MIX_LAYERS, MIX_OUT, D_MODEL), MIX_OUT ** -0.5),
        "w_pool_grp": nrm(ks[12], (N_MIX_LAYERS, N_POOL_GROUPS, POOL_GROUP, POOL_GROUP), POOL_GROUP ** -0.5),
        "pool_scale": 1.0 + nrm(ks[13], (N_MIX_LAYERS, POOL_WIDTH), 0.1),
        "conv_w": nrm(ks[14], (N_MIX_LAYERS, CONV_TAPS, CONV_WIDTH), CONV_TAPS ** -0.5),
        "conv_b": nrm(ks[15], (N_MIX_LAYERS, CONV_WIDTH), 0.02),
        "conv_ln_g": 1.0 + nrm(ks[16], (N_MIX_LAYERS, CONV_WIDTH), 0.05),
        "conv_ln_b": nrm(ks[17], (N_MIX_LAYERS, CONV_WIDTH), 0.02),
        "w_qkv": nrm(ks[18], (N_SB_LAYERS, D_MODEL, 3 * D_MODEL), D_MODEL ** -0.5),
        "w_o": nrm(ks[19], (N_SB_LAYERS, D_MODEL, D_MODEL), D_MODEL ** -0.5),
        "sb_bias": SB_BIAS_INIT + nrm(ks[22], (N_SB_LAYERS, N_HEADS), 0.1),
        "w_up": nrm(ks[20], (DEPTH, D_MODEL, D_FF), D_MODEL ** -0.5),
        "w_down": nrm(ks[21], (DEPTH, D_FF, D_MODEL), D_FF ** -0.5),
    }


def reference(x_prompt, x_sample, cache_k, cache_v, state_pool, state_conv, page_table,
              norm_mix, norm_mlp, norm_final, w_in_mix, w_out_mix, w_pool_grp, pool_scale,
              conv_w, conv_b, conv_ln_g, conv_ln_b, w_qkv, w_o, sb_bias, w_up, w_down):
    B, S, _ = x_prompt.shape
    DB, T, _ = x_sample.shape
    past_len = page_table.shape[1] * cache_k.shape[2]
    hp, hs = x_prompt, x_sample
    k_p, v_p, k_s, v_s = [], [], [], []
    pool_p, pool_s, conv_p, conv_s = [], [], [], []
    for l in range(DEPTH):
        if l % 2 == 0:
            m = l // 2
            wts = (norm_mix[l], w_in_mix[m], w_out_mix[m], w_pool_grp[m], pool_scale[m],
                   conv_w[m], conv_b[m], conv_ln_g[m], conv_ln_b[m])
            hp, np_pool, np_conv = mix_layer(
                hp, jnp.zeros((B, 0, POOL_WIDTH), hp.dtype), jnp.zeros((B, CONV_BUF, CONV_WIDTH), hp.dtype),
                0, *wts)
            hs, ns_pool, ns_conv = mix_layer(hs, state_pool[m].astype(hs.dtype), state_conv[m].astype(hs.dtype),
                                             past_len, *wts)
            pool_p.append(np_pool); pool_s.append(ns_pool)
            conv_p.append(np_conv); conv_s.append(ns_conv)
        else:
            a = l // 2
            q, k, v = sb_qkv(hp, norm_mix[l], w_qkv[a])
            hp = hp + sb_prompt(q, k, v, sb_bias[a]).reshape(B, S, D_MODEL) @ w_o[a]
            k_p.append(k); v_p.append(v)
            q, k, v = sb_qkv(hs, norm_mix[l], w_qkv[a])
            k_past = cache_k[a][page_table].reshape(DB, past_len, N_HEADS, HEAD_DIM).astype(k.dtype)
            v_past = cache_v[a][page_table].reshape(DB, past_len, N_HEADS, HEAD_DIM).astype(v.dtype)
            k_all = jnp.concatenate([k_past, k], axis=1)
            v_all = jnp.concatenate([v_past, v], axis=1)
            o = sb_attend(q, k_all, v_all, past_len + jnp.arange(T), jnp.arange(past_len + T), sb_bias[a])
            hs = hs + o.reshape(DB, T, D_MODEL) @ w_o[a]
            k_s.append(k); v_s.append(v)
        hp = mlp(hp, norm_mlp[l], w_up[l], w_down[l])
        hs = mlp(hs, norm_mlp[l], w_up[l], w_down[l])
    y_prompt = rmsnorm(hp, norm_final)
    y_sample = rmsnorm(hs, norm_final)
    k_prompt_new = jnp.stack(k_p)
    v_prompt_new = jnp.stack(v_p)
    k_sample_new = jnp.stack(k_s)
    v_sample_new = jnp.stack(v_s)
    pool_prompt_new = jnp.stack(pool_p)
    pool_sample_new = jnp.stack(pool_s)
    conv_prompt_new = jnp.stack(conv_p)
    conv_sample_new = jnp.stack(conv_s)
    return (y_prompt, y_sample, k_prompt_new, v_prompt_new, k_sample_new, v_sample_new,
            pool_prompt_new, pool_sample_new, conv_prompt_new, conv_sample_new)
```

```python
import functools

import jax
import jax.numpy as jnp
from jax import lax
from jax.experimental import pallas as pl
from jax.experimental.pallas import tpu as pltpu

F32 = jnp.float32
BF16 = jnp.bfloat16

EPS = 1e-6
POOL_WINDOWS = (2, 4, 8, 16)

V7X_SUBLANES = 8
V7X_LANES = 128
V7X_VMEM_BYTES = 64 * 1024 * 1024
VMEM_LIMIT_BYTES = V7X_VMEM_BYTES * 7 // 8

TOKEN_TILE = 512
MLP_CHUNK = 1024
MIX_CHUNK_VREGS = 16
ATTN_Q_TILE = 256
ATTN_K_TILE = 128
POOL_HALO = 16
SAMPLE_SEQS = 32


def _params(n_axes):
    return pltpu.CompilerParams(
        dimension_semantics=("arbitrary",) * n_axes,
        vmem_limit_bytes=VMEM_LIMIT_BYTES,
    )


def _resident(shape):
    zeros = (0,) * len(shape)
    return pl.BlockSpec(shape, lambda *_: zeros, pipeline_mode=pl.Buffered(1))


def _rmsnorm(x, g):
    return x * lax.rsqrt(jnp.mean(x * x, axis=-1, keepdims=True) + EPS) * g


def _sigmoid(x):
    return 1.0 / (1.0 + jnp.exp(-x))


def _dot(a, b):
    return jnp.dot(a, b, preferred_element_type=F32)


def _dot_nt(a, b):
    return lax.dot_general(a, b, (((1,), (1,)), ((), ())), preferred_element_type=F32)


def _mix_in_kernel(x_ref, g_ref, w_ref, o_ref, *, pool_w, conv_w):
    xn = _rmsnorm(x_ref[...], g_ref[...]).astype(BF16)
    o_ref[:, :pool_w] = _dot(xn, w_ref[:, :pool_w])
    a = _dot(xn, w_ref[:, pool_w:pool_w + conv_w])
    gate = _dot(xn, w_ref[:, pool_w + conv_w:])
    o_ref[:, pool_w:] = a * _sigmoid(gate)


def _mix_in(h, g, w):
    n, d = h.shape
    cols = w.shape[1]
    pool_w = conv_w = cols // 3
    tm = min(TOKEN_TILE, n)
    return pl.pallas_call(
        functools.partial(_mix_in_kernel, pool_w=pool_w, conv_w=conv_w),
        grid=(n // tm,),
        in_specs=[
            pl.BlockSpec((tm, d), lambda i: (i, 0)),
            _resident((1, d)),
            _resident((d, cols)),
        ],
        out_specs=pl.BlockSpec((tm, pool_w + conv_w), lambda i: (i, 0)),
        out_shape=jax.ShapeDtypeStruct((n, pool_w + conv_w), F32),
        compiler_params=_params(1),
        name="mix_in",
    )(h, g, w)


def _qkv_kernel(x_ref, g_ref, wq_ref, wk_ref, wv_ref, wkt_ref, wvt_ref, *refs, q_scale, has_alias):
    q_ref, kb_ref, vb_ref, ktb_ref, kt_ref, vt_ref = refs[2 if has_alias else 0:]
    xn = _rmsnorm(x_ref[...], g_ref[...]).astype(BF16)
    q_ref[...] = (_dot(xn, wq_ref[...]) * q_scale).astype(BF16)
    kb_ref[...] = _dot(xn, wk_ref[...]).astype(BF16)
    vb_ref[...] = _dot(xn, wv_ref[...]).astype(BF16)
    kt = _dot_nt(wkt_ref[...], xn)
    kt_ref[...] = kt
    ktb_ref[...] = kt.astype(BF16)
    vt_ref[...] = _dot_nt(wvt_ref[...], xn)


def _qkv(h, g, wq, wk, wv, wkt, wvt, q_scale, kt_all, vt_all, layer, n_layers, groups):
    n, d = h.shape
    glen = n // groups
    tm = min(TOKEN_TILE, glen)
    per_group = glen // tm
    tile = pl.BlockSpec((tm, d), lambda i: (i, 0))
    t_tile = pl.BlockSpec((None, None, d, tm), lambda i: (layer, i // per_group, 0, i % per_group))
    tb_tile = pl.BlockSpec((None, d, tm), lambda i: (i // per_group, 0, i % per_group))
    weights = [_resident((d, d))] * 5
    has_alias = kt_all is not None
    alias_args = (kt_all, vt_all) if has_alias else ()
    alias_specs = [pl.BlockSpec(memory_space=pl.ANY)] * len(alias_args)
    t_shape = jax.ShapeDtypeStruct((n_layers, groups, d, glen), F32)
    return pl.pallas_call(
        functools.partial(_qkv_kernel, q_scale=q_scale, has_alias=has_alias),
        grid=(n // tm,),
        in_specs=[tile, _resident((1, d))] + weights + alias_specs,
        out_specs=[tile, tile, tile, tb_tile, t_tile, t_tile],
        out_shape=[jax.ShapeDtypeStruct((n, d), BF16)] * 3
        + [jax.ShapeDtypeStruct((groups, d, glen), BF16), t_shape, t_shape],
        input_output_aliases={7: 4, 8: 5} if has_alias else {},
        compiler_params=_params(1),
        name="qkv",
    )(h, g, wq, wk, wv, wkt, wvt, *alias_args)


def _post_kernel(h_ref, a_ref, wo_ref, g_ref, wup_ref, wdn_ref, gf_ref, o_ref, *, n_chunks, final):
    h1 = h_ref[...] + _dot(a_ref[...].astype(BF16), wo_ref[...])
    xn = _rmsnorm(h1, g_ref[...]).astype(BF16)
    ck = wup_ref.shape[1] // n_chunks
    acc = h1
    for c in range(n_chunks):
        up = _dot(xn, wup_ref[:, c * ck:(c + 1) * ck])
        act = jnp.square(jnp.maximum(up, 0.0)).astype(BF16)
        acc = acc + _dot(act, wdn_ref[c * ck:(c + 1) * ck, :])
    if final:
        acc = _rmsnorm(acc, gf_ref[...])
    o_ref[...] = acc


def _post(h, a, wo, g, wup, wdn, g_final, final):
    n, d = h.shape
    dff = wup.shape[1]
    tm = min(TOKEN_TILE, n)
    tile = pl.BlockSpec((tm, d), lambda i: (i, 0))
    return pl.pallas_call(
        functools.partial(_post_kernel, n_chunks=max(1, dff // MLP_CHUNK), final=final),
        grid=(n // tm,),
        in_specs=[
            tile,
            pl.BlockSpec((tm, a.shape[1]), lambda i: (i, 0)),
            _resident(wo.shape),
            _resident((1, d)),
            _resident((d, dff)),
            _resident((dff, d)),
            _resident((1, d)),
        ],
        out_specs=tile,
        out_shape=jax.ShapeDtypeStruct((n, d), F32),
        compiler_params=_params(1),
        name="post_mlp",
    )(h, a, wo, g, wup, wdn, g_final)


def _mix_core_kernel(ug_ref, hp_ref, hc_ref, wg_ref, sc_ref, cw_ref, cb_ref, lg_ref, lb_ref,
                     o_ref, uf_ref, gf_ref, d_ref, *, time_axis, nb, tt, pos0, n_t, conv_halo):
    ti = pl.program_id(1)
    pool_w = hp_ref.shape[-1]
    conv_w = hc_ref.shape[-1]
    n_pool_hist = hp_ref.shape[time_axis]
    n_conv_hist = hc_ref.shape[time_axis]
    taps = cw_ref.shape[0]
    group_w = pool_w // len(POOL_WINDOWS)

    def rows(start, size, lanes=slice(None)):
        t = slice(start, start + size)
        return (t, slice(None), lanes) if time_axis == 0 else (slice(None), t, lanes)

    def shape(n_rows, width):
        return (n_rows, nb, width) if time_axis == 0 else (nb, n_rows, width)

    if time_axis == 0:
        rc = max(1, MIX_CHUNK_VREGS * V7X_SUBLANES * V7X_LANES // (nb * conv_w))
    else:
        rc = MIX_CHUNK_VREGS * V7X_SUBLANES * V7X_LANES // conv_w
    rc = min(rc, tt)
    row_iota_shape = (rc, 1, 1) if time_axis == 0 else (1, rc, 1)

    @pl.when(ti == 0)
    def _load_history():
        uf_ref[rows(0, POOL_HALO - n_pool_hist)] = jnp.zeros(shape(POOL_HALO - n_pool_hist, pool_w), F32)
        uf_ref[rows(POOL_HALO - n_pool_hist, n_pool_hist)] = hp_ref[...]
        gf_ref[rows(0, conv_halo - n_conv_hist)] = jnp.zeros(shape(conv_halo - n_conv_hist, conv_w), F32)
        gf_ref[rows(conv_halo - n_conv_hist, n_conv_hist)] = hc_ref[...]

    uf_ref[rows(POOL_HALO, tt)] = ug_ref[:, :, :pool_w]
    gf_ref[rows(conv_halo, tt)] = ug_ref[:, :, pool_w:]

    cb = cb_ref[...].reshape(1, 1, conv_w)
    lg = lg_ref[...].reshape(1, 1, conv_w)
    lb = lb_ref[...].reshape(1, 1, conv_w)

    for r0 in range(0, tt, rc):
        pos = pos0 + ti * tt + r0 + lax.broadcasted_iota(jnp.int32, row_iota_shape, time_axis)
        for gi, win in enumerate(POOL_WINDOWS):
            lanes = slice(gi * group_w, (gi + 1) * group_w)
            u_new = uf_ref[rows(POOL_HALO + r0, rc, lanes)]
            s = u_new
            for i in range(1, win):
                s = s + uf_ref[rows(POOL_HALO + r0 - i, rc, lanes)]
            cnt = jnp.minimum(win, pos + 1).astype(F32)
            d_ref[rows(r0, rc, lanes)] = s / cnt - u_new

        base = conv_halo - n_conv_hist + r0
        acc = jnp.zeros(shape(rc, conv_w), F32)
        for k in range(taps):
            acc = acc + gf_ref[rows(base + k, rc)] * cw_ref[k:k + 1, :].reshape(1, 1, conv_w)
        y = acc + cb
        mu = jnp.mean(y, axis=-1, keepdims=True)
        yc = y - mu
        var = jnp.mean(yc * yc, axis=-1, keepdims=True)
        yn = yc * lax.rsqrt(var + EPS) * lg + lb
        o_ref[rows(r0, rc, slice(pool_w, pool_w + conv_w))] = yn * _sigmoid(yn)

    for gi in range(len(POOL_WINDOWS)):
        lanes = slice(gi * group_w, (gi + 1) * group_w)
        d = d_ref[:, :, lanes].reshape(nb * tt, group_w).astype(BF16)
        y = _dot(d, wg_ref[gi]) * sc_ref[:, lanes]
        o_ref[:, :, lanes] = y.reshape(shape(tt, group_w))

    if n_t > 1:
        uf_ref[rows(0, POOL_HALO)] = uf_ref[rows(tt, POOL_HALO)]
        gf_ref[rows(0, conv_halo)] = gf_ref[rows(tt, conv_halo)]


def _mix_core(ug, hist_pool, hist_conv, w_grp, scale, conv_w, conv_b, ln_g, ln_b, *,
              time_axis, nb, tt, pos0):
    seq_axis = 1 - time_axis
    n_seq, t, width = ug.shape[seq_axis], ug.shape[time_axis], ug.shape[2]
    pool_w = hist_pool.shape[-1]
    cw = hist_conv.shape[-1]
    n_t = t // tt
    conv_halo = -(-hist_conv.shape[time_axis] // V7X_SUBLANES) * V7X_SUBLANES

    def block(n_rows, c):
        if time_axis == 0:
            return pl.BlockSpec((n_rows, nb, c), lambda b, i: (i if n_rows == tt else 0, b, 0))
        return pl.BlockSpec((nb, n_rows, c), lambda b, i: (b, i if n_rows == tt else 0, 0))

    def scratch(n_rows, c):
        return pltpu.VMEM((n_rows, nb, c) if time_axis == 0 else (nb, n_rows, c), F32)

    return pl.pallas_call(
        functools.partial(_mix_core_kernel, time_axis=time_axis, nb=nb, tt=tt, pos0=pos0, n_t=n_t,
                          conv_halo=conv_halo),
        grid=(n_seq // nb, n_t),
        in_specs=[
            block(tt, width),
            block(hist_pool.shape[time_axis], pool_w),
            block(hist_conv.shape[time_axis], cw),
            _resident(w_grp.shape),
            _resident(scale.shape),
            _resident(conv_w.shape),
            _resident(conv_b.shape),
            _resident(ln_g.shape),
            _resident(ln_b.shape),
        ],
        out_specs=block(tt, width),
        out_shape=jax.ShapeDtypeStruct(ug.shape, F32),
        scratch_shapes=[scratch(POOL_HALO + tt, pool_w), scratch(conv_halo + tt, cw), scratch(tt, pool_w)],
        compiler_params=_params(2),
        name="mix_core",
    )(ug, hist_pool, hist_conv, w_grp, scale, conv_w, conv_b, ln_g, ln_b)


def _sb_weights(z, carry, w, mask):
    tk = z.shape[1]
    log_term = jnp.log1p(jnp.exp(-jnp.abs(z)))
    sp = jnp.maximum(z, 0.0) + log_term
    log_beta = jnp.minimum(z, 0.0) - log_term
    if mask is not None:
        sp = jnp.where(mask, sp, 0.0)
    hi = sp.astype(BF16)
    lo = (sp - hi.astype(F32)).astype(BF16)
    sums = _dot(jnp.concatenate([hi, lo], axis=1), w)
    a = jnp.exp(log_beta - (sums[:, :tk] + carry))
    if mask is not None:
        a = jnp.where(mask, a, 0.0)
    return carry + sums[:, tk:], a


def _sb_prompt_kernel(bias_ref, q_ref, kt_ref, v_ref, w_ref, o_ref, *, tq, tk, hd):
    col_block = pl.program_id(1)
    qi = pl.program_id(2)
    n_diag = tq // tk
    width = q_ref.shape[-1]
    heads = width // hd
    lane = lax.broadcasted_iota(jnp.int32, (tq, width), 1)
    row = lax.broadcasted_iota(jnp.int32, (tq, tk), 0)
    col = lax.broadcasted_iota(jnp.int32, (tq, tk), 1)
    q = q_ref[...].astype(F32)
    w = w_ref[...]
    out = jnp.zeros((tq, width), F32)
    for hh in range(heads):
        in_head = (lane >= hh * hd) & (lane < (hh + 1) * hd)
        qh = jnp.where(in_head, q, 0.0).astype(BF16)
        bias = bias_ref[col_block * heads + hh]

        def block(kb, carry, acc, mask):
            start = pl.multiple_of(kb * tk, tk)
            z = _dot(qh, kt_ref[:, pl.ds(start, tk)]) + bias
            carry, a = _sb_weights(z, carry, w, mask)
            return carry, acc + _dot(a.astype(BF16), v_ref[pl.ds(start, tk), :])

        carry = jnp.zeros((tq, tk), F32)
        acc = jnp.zeros((tq, width), F32)
        for dblk in reversed(range(n_diag)):
            carry, acc = block(qi * n_diag + dblk, carry, acc, col + dblk * tk < row)
        carry, acc = lax.fori_loop(
            0, qi * n_diag, lambda j, s: block(qi * n_diag - 1 - j, s[0], s[1], None), (carry, acc))
        out = jnp.where(in_head, acc, out)
    o_ref[...] = out


def _sb_prompt(q, kt, v, bias, w, *, hd):
    b, s, d = q.shape
    tq = min(ATTN_Q_TILE, s)
    tk = ATTN_K_TILE
    qtile = pl.BlockSpec((None, tq, V7X_LANES), lambda bi, p, i, *_: (bi, i, p))
    return pl.pallas_call(
        functools.partial(_sb_prompt_kernel, tq=tq, tk=tk, hd=hd),
        grid_spec=pltpu.PrefetchScalarGridSpec(
            num_scalar_prefetch=1,
            grid=(b, d // V7X_LANES, s // tq),
            in_specs=[
                qtile,
                pl.BlockSpec((None, V7X_LANES, s), lambda bi, p, i, *_: (bi, p, 0)),
                pl.BlockSpec((None, s, V7X_LANES), lambda bi, p, i, *_: (bi, 0, p)),
                pl.BlockSpec(w.shape, lambda *_: (0, 0)),
            ],
            out_specs=qtile,
        ),
        out_shape=jax.ShapeDtypeStruct((b, s, d), F32),
        compiler_params=_params(3),
        name="sb_prompt",
    )(bias, q, kt, v, w)


def _sb_sample_kernel(pt_ref, q_ref, kn_ref, vn_ref, kc_ref, vc_ref, w_ref, b_ref, o_ref,
                      qbd_ref, carry_ref, new_ref, acc_ref, *, n_heads, hd, t_new):
    j = pl.program_id(1)
    rows = n_heads * t_new
    page = kc_ref.shape[-1]
    d = q_ref.shape[-1]
    w = w_ref[...]
    bias = b_ref[...]

    @pl.when(j == 0)
    def _new_keys():
        q = jnp.broadcast_to(q_ref[...][None], (n_heads, t_new, d)).reshape(rows, d)
        r = lax.broadcasted_iota(jnp.int32, (rows, d), 0)
        c = lax.broadcasted_iota(jnp.int32, (rows, d), 1)
        qbd = jnp.where(r // t_new == c // hd, q, 0.0).astype(BF16)
        qbd_ref[...] = qbd
        pad = jnp.zeros((page - t_new, d), F32)
        kn = jnp.concatenate([kn_ref[...], pad], axis=0).astype(BF16)
        vn = jnp.concatenate([vn_ref[...], pad], axis=0).astype(BF16)
        rr = lax.broadcasted_iota(jnp.int32, (rows, page), 0)
        cc = lax.broadcasted_iota(jnp.int32, (rows, page), 1)
        z = _dot_nt(qbd, kn) + bias
        carry, a = _sb_weights(z, jnp.zeros((rows, page), F32), w, cc < rr % t_new)
        carry_ref[...] = carry
        new_ref[...] = _dot(a.astype(BF16), vn)
        acc_ref[...] = jnp.zeros(acc_ref.shape, F32)

    z = _dot(qbd_ref[...], kc_ref[...].astype(BF16)) + bias
    carry, a = _sb_weights(z, carry_ref[...], w, None)
    carry_ref[...] = carry
    acc_ref[...] += _dot(vc_ref[...].astype(BF16), a.T.astype(BF16))

    @pl.when(j == pl.num_programs(1) - 1)
    def _emit():
        total = new_ref[...] + acc_ref[...].T
        head_of_col = lax.broadcasted_iota(jnp.int32, (t_new, d), 1) // hd
        out = jnp.zeros((t_new, d), F32)
        for h in range(n_heads):
            out = jnp.where(head_of_col == h, total[h * t_new:(h + 1) * t_new, :], out)
        o_ref[...] = out


def _sb_sample(page_table, q, k_new, v_new, cache_kt, cache_vt, layer, bias_rows, w, *, n_heads, hd):
    db, t_new, d = q.shape
    n_pages = page_table.shape[1]
    page = cache_kt.shape[-1]
    rows = n_heads * t_new
    seq = pl.BlockSpec((None, t_new, d), lambda b, j, pt: (b, 0, 0))
    cache = pl.BlockSpec((None, None, d, page), lambda b, j, pt: (layer, pt[b, n_pages - 1 - j], 0, 0))
    return pl.pallas_call(
        functools.partial(_sb_sample_kernel, n_heads=n_heads, hd=hd, t_new=t_new),
        grid_spec=pltpu.PrefetchScalarGridSpec(
            num_scalar_prefetch=1,
            grid=(db, n_pages),
            in_specs=[
                seq, seq, seq, cache, cache,
                pl.BlockSpec(w.shape, lambda *_: (0, 0)),
                pl.BlockSpec(bias_rows.shape, lambda *_: (0, 0)),
            ],
            out_specs=seq,
            scratch_shapes=[
                pltpu.VMEM((rows, d), BF16),
                pltpu.VMEM((rows, page), F32),
                pltpu.VMEM((rows, d), F32),
                pltpu.VMEM((d, rows), F32),
            ],
        ),
        out_shape=jax.ShapeDtypeStruct((db, t_new, d), F32),
        compiler_params=_params(2),
        name="sb_sample",
    )(page_table, q, k_new, v_new, cache_kt, cache_vt, w, bias_rows)


def _suffix_sum_matrix(tk):
    j = lax.broadcasted_iota(jnp.int32, (tk, tk), 0)
    s = lax.broadcasted_iota(jnp.int32, (tk, tk), 1)
    half = jnp.concatenate([(j > s).astype(BF16), jnp.ones((tk, tk), BF16)], axis=1)
    return jnp.concatenate([half, half], axis=0)


def kernel(x_prompt, x_sample, cache_k, cache_v, state_pool, state_conv, page_table, norm_mix, norm_mlp, norm_final, w_in_mix, w_out_mix, w_pool_grp, pool_scale, conv_w, conv_b, conv_ln_g, conv_ln_b, w_qkv, w_o, sb_bias, w_up, w_down):
    b, s, d = x_prompt.shape
    db, t_new, _ = x_sample.shape
    depth = norm_mix.shape[0]
    n_sb, n_pool_pages, page, n_heads, hd = cache_k.shape
    past_len = page_table.shape[1] * page
    pool_w = state_pool.shape[-1]
    conv_width = state_conv.shape[-1]
    n_pool_hist = state_pool.shape[2]
    n_conv_hist = state_conv.shape[2]
    q_scale = float(hd) ** -0.5

    row = lambda x: x.reshape(1, -1)
    hp = x_prompt.reshape(b * s, d)
    hs = x_sample.transpose(1, 0, 2).reshape(t_new * db, d)
    cache_kt = cache_k.transpose(0, 1, 3, 4, 2).reshape(n_sb, n_pool_pages, d, page)
    cache_vt = cache_v.transpose(0, 1, 3, 4, 2).reshape(n_sb, n_pool_pages, d, page)
    state_pool_t = state_pool.transpose(0, 2, 1, 3)
    state_conv_t = state_conv.transpose(0, 2, 1, 3)
    w_sums = _suffix_sum_matrix(ATTN_K_TILE)
    g_final = row(norm_final)

    kt_p = vt_p = kt_s = vt_s = None
    pool_p, pool_s, conv_p, conv_s = [], [], [], []
    for l in range(depth):
        final = l == depth - 1
        g_mix = row(norm_mix[l])
        mlp_w = (row(norm_mlp[l]), w_up[l].astype(BF16), w_down[l].astype(BF16), g_final, final)
        if l % 2 == 0:
            m = l // 2
            w_in = w_in_mix[m].astype(BF16)
            w_out = w_out_mix[m].astype(BF16)
            core_w = (w_pool_grp[m].astype(BF16), row(pool_scale[m]), conv_w[m], row(conv_b[m]),
                      row(conv_ln_g[m]), row(conv_ln_b[m]))

            ug_p = _mix_in(hp, g_mix, w_in).reshape(b, s, pool_w + conv_width)
            ug_s = _mix_in(hs, g_mix, w_in).reshape(t_new, db, pool_w + conv_width)
            cat_p = _mix_core(ug_p, jnp.zeros((b, n_pool_hist, pool_w), F32),
                              jnp.zeros((b, n_conv_hist, conv_width), F32), *core_w,
                              time_axis=1, nb=1, tt=min(TOKEN_TILE, s), pos0=0)
            cat_s = _mix_core(ug_s, state_pool_t[m], state_conv_t[m], *core_w,
                              time_axis=0, nb=min(SAMPLE_SEQS, db), tt=t_new, pos0=past_len)
            pool_p.append(ug_p[:, s - n_pool_hist:, :pool_w])
            conv_p.append(ug_p[:, s - n_conv_hist:, pool_w:])
            pool_s.append(jnp.concatenate([state_pool_t[m], ug_s[:, :, :pool_w]], axis=0)[-n_pool_hist:])
            conv_s.append(jnp.concatenate([state_conv_t[m], ug_s[:, :, pool_w:]], axis=0)[-n_conv_hist:])
            hp = _post(hp, cat_p.reshape(b * s, -1), w_out, *mlp_w)
            hs = _post(hs, cat_s.reshape(t_new * db, -1), w_out, *mlp_w)
        else:
            a = l // 2
            wq, wk, wv = (w_qkv[a][:, i * d:(i + 1) * d].astype(BF16) for i in range(3))
            qkv_w = (wq, wk, wv, wk.T, wv.T, q_scale)
            wo = w_o[a].astype(BF16)

            q, _, vb, ktb, kt_p, vt_p = _qkv(hp, g_mix, *qkv_w, kt_p, vt_p, a, n_sb, b)
            o_p = _sb_prompt(q.reshape(b, s, d), ktb, vb.reshape(b, s, d), sb_bias[a], w_sums, hd=hd)
            hp = _post(hp, o_p.reshape(b * s, d), wo, *mlp_w)

            q, kb, vb, _, kt_s, vt_s = _qkv(hs, g_mix, *qkv_w, kt_s, vt_s, a, n_sb, t_new)
            by_seq = lambda x: x.reshape(t_new, db, d).transpose(1, 0, 2).astype(F32)
            bias_rows = jnp.broadcast_to(jnp.repeat(sb_bias[a], t_new)[:, None], (n_heads * t_new, page))
            o_s = _sb_sample(page_table, by_seq(q), by_seq(kb), by_seq(vb), cache_kt, cache_vt, a,
                             bias_rows, w_sums, n_heads=n_heads, hd=hd)
            hs = _post(hs, o_s.transpose(1, 0, 2).reshape(t_new * db, d), wo, *mlp_w)

    kv_prompt = lambda x: x.reshape(n_sb, b, n_heads, hd, s).transpose(0, 1, 4, 2, 3)
    kv_sample = lambda x: x.reshape(n_sb, t_new, n_heads, hd, db).transpose(0, 4, 1, 2, 3)
    state_sample = lambda xs: jnp.stack(xs).transpose(0, 2, 1, 3)
    return (hp.reshape(b, s, d), hs.reshape(t_new, db, d).transpose(1, 0, 2),
            kv_prompt(kt_p), kv_prompt(vt_p), kv_sample(kt_s), kv_sample(vt_s),
            jnp.stack(pool_p), state_sample(pool_s), jnp.stack(conv_p), state_sample(conv_s))
```

```python
import functools

import jax
import jax.numpy as jnp
from jax import lax
from jax.experimental import pallas as pl
from jax.experimental.pallas import tpu as pltpu

F32 = jnp.float32
BF16 = jnp.bfloat16

EPS = 1e-6
POOL_WINDOWS = (2, 4, 8, 16)

V7X_SUBLANES = 8
V7X_LANES = 128
V7X_VMEM_BYTES = 64 * 1024 * 1024
VMEM_LIMIT_BYTES = V7X_VMEM_BYTES * 7 // 8

TOKEN_TILE = 512
MLP_CHUNK = 1024
MIX_CHUNK_VREGS = 16
ATTN_Q_TILE = 512
ATTN_K_BLOCK = 256
SAMPLE_PAGES_PER_STEP = 8
LOG2E = 1.4426950408889634
POOL_HALO = 16
SAMPLE_SEQS = 32


def _params(n_axes):
    return pltpu.CompilerParams(
        dimension_semantics=("arbitrary",) * n_axes,
        vmem_limit_bytes=VMEM_LIMIT_BYTES,
    )


def _resident(shape):
    zeros = (0,) * len(shape)
    return pl.BlockSpec(shape, lambda *_: zeros, pipeline_mode=pl.Buffered(1))


def _rmsnorm(x, g):
    return x * lax.rsqrt(jnp.mean(x * x, axis=-1, keepdims=True) + EPS) * g


def _sigmoid(x):
    return 1.0 / (1.0 + jnp.exp(-x))


def _dot(a, b):
    return jnp.dot(a, b, preferred_element_type=F32)


def _dot_nt(a, b):
    return lax.dot_general(a, b, (((1,), (1,)), ((), ())), preferred_element_type=F32)


def _mix_in_kernel(x_ref, g_ref, w_ref, o_ref, *, pool_w, conv_w):
    xn = _rmsnorm(x_ref[...], g_ref[...]).astype(BF16)
    o_ref[:, :pool_w] = _dot(xn, w_ref[:, :pool_w])
    a = _dot(xn, w_ref[:, pool_w:pool_w + conv_w])
    gate = _dot(xn, w_ref[:, pool_w + conv_w:])
    o_ref[:, pool_w:] = a * _sigmoid(gate)


def _mix_in(h, g, w):
    n, d = h.shape
    cols = w.shape[1]
    pool_w = conv_w = cols // 3
    tm = min(TOKEN_TILE, n)
    return pl.pallas_call(
        functools.partial(_mix_in_kernel, pool_w=pool_w, conv_w=conv_w),
        grid=(n // tm,),
        in_specs=[
            pl.BlockSpec((tm, d), lambda i: (i, 0)),
            _resident((1, d)),
            _resident((d, cols)),
        ],
        out_specs=pl.BlockSpec((tm, pool_w + conv_w), lambda i: (i, 0)),
        out_shape=jax.ShapeDtypeStruct((n, pool_w + conv_w), F32),
        compiler_params=_params(1),
        name="mix_in",
    )(h, g, w)


def _qkv_kernel(x_ref, g_ref, wq_ref, wk_ref, wv_ref, wkt_ref, wvt_ref, *refs, q_scale, has_alias):
    q_ref, kb_ref, vb_ref, ktb_ref, kt_ref, vt_ref = refs[2 if has_alias else 0:]
    xn = _rmsnorm(x_ref[...], g_ref[...]).astype(BF16)
    q_ref[...] = (_dot(xn, wq_ref[...]) * q_scale).astype(BF16)
    kb_ref[...] = _dot(xn, wk_ref[...]).astype(BF16)
    vb_ref[...] = _dot(xn, wv_ref[...]).astype(BF16)
    kt = _dot_nt(wkt_ref[...], xn)
    kt_ref[...] = kt
    ktb_ref[...] = kt.astype(BF16)
    vt_ref[...] = _dot_nt(wvt_ref[...], xn)


def _qkv(h, g, wq, wk, wv, wkt, wvt, q_scale, kt_all, vt_all, layer, n_layers, groups):
    n, d = h.shape
    glen = n // groups
    tm = min(TOKEN_TILE, glen)
    per_group = glen // tm
    tile = pl.BlockSpec((tm, d), lambda i: (i, 0))
    t_tile = pl.BlockSpec((None, None, d, tm), lambda i: (layer, i // per_group, 0, i % per_group))
    tb_tile = pl.BlockSpec((None, d, tm), lambda i: (i // per_group, 0, i % per_group))
    weights = [_resident((d, d))] * 5
    has_alias = kt_all is not None
    alias_args = (kt_all, vt_all) if has_alias else ()
    alias_specs = [pl.BlockSpec(memory_space=pl.ANY)] * len(alias_args)
    t_shape = jax.ShapeDtypeStruct((n_layers, groups, d, glen), F32)
    return pl.pallas_call(
        functools.partial(_qkv_kernel, q_scale=q_scale, has_alias=has_alias),
        grid=(n // tm,),
        in_specs=[tile, _resident((1, d))] + weights + alias_specs,
        out_specs=[tile, tile, tile, tb_tile, t_tile, t_tile],
        out_shape=[jax.ShapeDtypeStruct((n, d), BF16)] * 3
        + [jax.ShapeDtypeStruct((groups, d, glen), BF16), t_shape, t_shape],
        input_output_aliases={7: 4, 8: 5} if has_alias else {},
        compiler_params=_params(1),
        name="qkv",
    )(h, g, wq, wk, wv, wkt, wvt, *alias_args)


def _post_kernel(h_ref, a_ref, wo_ref, g_ref, wup_ref, wdn_ref, gf_ref, o_ref, *, n_chunks, final):
    h1 = h_ref[...] + _dot(a_ref[...].astype(BF16), wo_ref[...])
    xn = _rmsnorm(h1, g_ref[...]).astype(BF16)
    ck = wup_ref.shape[1] // n_chunks
    acc = h1
    for c in range(n_chunks):
        up = _dot(xn, wup_ref[:, c * ck:(c + 1) * ck])
        act = jnp.square(jnp.maximum(up, 0.0)).astype(BF16)
        acc = acc + _dot(act, wdn_ref[c * ck:(c + 1) * ck, :])
    if final:
        acc = _rmsnorm(acc, gf_ref[...])
    o_ref[...] = acc


def _post(h, a, wo, g, wup, wdn, g_final, final):
    n, d = h.shape
    dff = wup.shape[1]
    tm = min(TOKEN_TILE, n)
    tile = pl.BlockSpec((tm, d), lambda i: (i, 0))
    return pl.pallas_call(
        functools.partial(_post_kernel, n_chunks=max(1, dff // MLP_CHUNK), final=final),
        grid=(n // tm,),
        in_specs=[
            tile,
            pl.BlockSpec((tm, a.shape[1]), lambda i: (i, 0)),
            _resident(wo.shape),
            _resident((1, d)),
            _resident((d, dff)),
            _resident((dff, d)),
            _resident((1, d)),
        ],
        out_specs=tile,
        out_shape=jax.ShapeDtypeStruct((n, d), F32),
        compiler_params=_params(1),
        name="post_mlp",
    )(h, a, wo, g, wup, wdn, g_final)


def _mix_core_kernel(ug_ref, hp_ref, hc_ref, wg_ref, sc_ref, cw_ref, cb_ref, lg_ref, lb_ref,
                     o_ref, uf_ref, gf_ref, d_ref, *, time_axis, nb, tt, pos0, n_t, conv_halo):
    ti = pl.program_id(1)
    pool_w = hp_ref.shape[-1]
    conv_w = hc_ref.shape[-1]
    n_pool_hist = hp_ref.shape[time_axis]
    n_conv_hist = hc_ref.shape[time_axis]
    taps = cw_ref.shape[0]
    group_w = pool_w // len(POOL_WINDOWS)

    def rows(start, size, lanes=slice(None)):
        t = slice(start, start + size)
        return (t, slice(None), lanes) if time_axis == 0 else (slice(None), t, lanes)

    def shape(n_rows, width):
        return (n_rows, nb, width) if time_axis == 0 else (nb, n_rows, width)

    if time_axis == 0:
        rc = max(1, MIX_CHUNK_VREGS * V7X_SUBLANES * V7X_LANES // (nb * conv_w))
    else:
        rc = MIX_CHUNK_VREGS * V7X_SUBLANES * V7X_LANES // conv_w
    rc = min(rc, tt)
    row_iota_shape = (rc, 1, 1) if time_axis == 0 else (1, rc, 1)

    @pl.when(ti == 0)
    def _load_history():
        uf_ref[rows(0, POOL_HALO - n_pool_hist)] = jnp.zeros(shape(POOL_HALO - n_pool_hist, pool_w), F32)
        uf_ref[rows(POOL_HALO - n_pool_hist, n_pool_hist)] = hp_ref[...]
        gf_ref[rows(0, conv_halo - n_conv_hist)] = jnp.zeros(shape(conv_halo - n_conv_hist, conv_w), F32)
        gf_ref[rows(conv_halo - n_conv_hist, n_conv_hist)] = hc_ref[...]

    uf_ref[rows(POOL_HALO, tt)] = ug_ref[:, :, :pool_w]
    gf_ref[rows(conv_halo, tt)] = ug_ref[:, :, pool_w:]

    cb = cb_ref[...].reshape(1, 1, conv_w)
    lg = lg_ref[...].reshape(1, 1, conv_w)
    lb = lb_ref[...].reshape(1, 1, conv_w)

    for r0 in range(0, tt, rc):
        pos = pos0 + ti * tt + r0 + lax.broadcasted_iota(jnp.int32, row_iota_shape, time_axis)
        for gi, win in enumerate(POOL_WINDOWS):
            lanes = slice(gi * group_w, (gi + 1) * group_w)
            u_new = uf_ref[rows(POOL_HALO + r0, rc, lanes)]
            s = u_new
            for i in range(1, win):
                s = s + uf_ref[rows(POOL_HALO + r0 - i, rc, lanes)]
            cnt = jnp.minimum(win, pos + 1).astype(F32)
            d_ref[rows(r0, rc, lanes)] = s / cnt - u_new

        base = conv_halo - n_conv_hist + r0
        acc = jnp.zeros(shape(rc, conv_w), F32)
        for k in range(taps):
            acc = acc + gf_ref[rows(base + k, rc)] * cw_ref[k:k + 1, :].reshape(1, 1, conv_w)
        y = acc + cb
        mu = jnp.mean(y, axis=-1, keepdims=True)
        yc = y - mu
        var = jnp.mean(yc * yc, axis=-1, keepdims=True)
        yn = yc * lax.rsqrt(var + EPS) * lg + lb
        o_ref[rows(r0, rc, slice(pool_w, pool_w + conv_w))] = yn * _sigmoid(yn)

    for gi in range(len(POOL_WINDOWS)):
        lanes = slice(gi * group_w, (gi + 1) * group_w)
        d = d_ref[:, :, lanes].reshape(nb * tt, group_w).astype(BF16)
        y = _dot(d, wg_ref[gi]) * sc_ref[:, lanes]
        o_ref[:, :, lanes] = y.reshape(shape(tt, group_w))

    if n_t > 1:
        uf_ref[rows(0, POOL_HALO)] = uf_ref[rows(tt, POOL_HALO)]
        gf_ref[rows(0, conv_halo)] = gf_ref[rows(tt, conv_halo)]


def _mix_core(ug, hist_pool, hist_conv, w_grp, scale, conv_w, conv_b, ln_g, ln_b, *,
              time_axis, nb, tt, pos0):
    seq_axis = 1 - time_axis
    n_seq, t, width = ug.shape[seq_axis], ug.shape[time_axis], ug.shape[2]
    pool_w = hist_pool.shape[-1]
    cw = hist_conv.shape[-1]
    n_t = t // tt
    conv_halo = -(-hist_conv.shape[time_axis] // V7X_SUBLANES) * V7X_SUBLANES

    def block(n_rows, c):
        if time_axis == 0:
            return pl.BlockSpec((n_rows, nb, c), lambda b, i: (i if n_rows == tt else 0, b, 0))
        return pl.BlockSpec((nb, n_rows, c), lambda b, i: (b, i if n_rows == tt else 0, 0))

    def scratch(n_rows, c):
        return pltpu.VMEM((n_rows, nb, c) if time_axis == 0 else (nb, n_rows, c), F32)

    return pl.pallas_call(
        functools.partial(_mix_core_kernel, time_axis=time_axis, nb=nb, tt=tt, pos0=pos0, n_t=n_t,
                          conv_halo=conv_halo),
        grid=(n_seq // nb, n_t),
        in_specs=[
            block(tt, width),
            block(hist_pool.shape[time_axis], pool_w),
            block(hist_conv.shape[time_axis], cw),
            _resident(w_grp.shape),
            _resident(scale.shape),
            _resident(conv_w.shape),
            _resident(conv_b.shape),
            _resident(ln_g.shape),
            _resident(ln_b.shape),
        ],
        out_specs=block(tt, width),
        out_shape=jax.ShapeDtypeStruct(ug.shape, F32),
        scratch_shapes=[scratch(POOL_HALO + tt, pool_w), scratch(conv_halo + tt, cw), scratch(tt, pool_w)],
        compiler_params=_params(2),
        name="mix_core",
    )(ug, hist_pool, hist_conv, w_grp, scale, conv_w, conv_b, ln_g, ln_b)


def _sb_scores(z, mask):
    sign = jnp.uint32(0x80000000)
    neg_abs = lax.bitcast_convert_type(lax.bitcast_convert_type(z, jnp.uint32) | sign, F32)
    sp = jnp.maximum(z, 0.0) + jnp.log(1.0 + jnp.exp2(neg_abs)) * LOG2E
    log_beta = z - sp
    if mask is not None:
        sp = jnp.where(mask, sp, 0.0)
        log_beta = jnp.where(mask, log_beta, -jnp.inf)
    return sp.astype(BF16), log_beta, sp[:, :1]


def _sb_apply(scores, carry, w):
    sp, log_beta, sp_first = scores
    later = _dot(sp, w) + carry
    return later[:, :1] + sp_first, jnp.exp2(log_beta - later)


def _sb_prompt_kernel(bias_ref, q_ref, kt_ref, v_ref, w_ref, o_ref, *, tq, kb, hd):
    col_block = pl.program_id(1)
    qi = pl.program_id(2)
    n_diag = tq // kb
    width = q_ref.shape[-1]
    heads = width // hd
    lane = lax.broadcasted_iota(jnp.int32, (tq, width), 1)
    row = lax.broadcasted_iota(jnp.int32, (tq, kb), 0)
    col = lax.broadcasted_iota(jnp.int32, (tq, kb), 1)
    q = q_ref[...].astype(F32)
    w = w_ref[...]
    in_head = [(lane >= hh * hd) & (lane < (hh + 1) * hd) for hh in range(heads)]
    qh = [jnp.where(m, q, 0.0).astype(BF16) for m in in_head]
    bias = [bias_ref[col_block * heads + hh] for hh in range(heads)]

    def scores_of(blk, mask):
        kt = kt_ref[:, pl.ds(pl.multiple_of(blk * kb, kb), kb)]
        return tuple(_sb_scores(_dot(qh[hh], kt) + bias[hh], mask) for hh in range(heads))

    def accumulate(blk, scores, state):
        v = v_ref[pl.ds(pl.multiple_of(blk * kb, kb), kb), :]
        new_state = []
        for hh, (carry, acc) in enumerate(state):
            carry, a = _sb_apply(scores[hh], carry, w)
            new_state.append((carry, acc + _dot(a.astype(BF16), v)))
        return tuple(new_state)

    first = qi * n_diag
    state = tuple((jnp.zeros((tq, 1), F32), jnp.zeros((tq, width), F32)) for _ in range(heads))
    for dblk in reversed(range(n_diag)):
        state = accumulate(first + dblk, scores_of(first + dblk, col + dblk * kb < row), state)
    state = lax.fori_loop(
        0, first, lambda j, s: accumulate(first - 1 - j, scores_of(first - 1 - j, None), s), state)
    out = state[0][1]
    for hh in range(1, heads):
        out = jnp.where(in_head[hh], state[hh][1], out)
    o_ref[...] = out


def _sb_prompt(q, kt, v, bias, w, *, hd):
    b, s, d = q.shape
    tq = min(ATTN_Q_TILE, s)
    kb = w.shape[0]
    qtile = pl.BlockSpec((None, tq, V7X_LANES), lambda bi, p, i, *_: (bi, i, p))
    return pl.pallas_call(
        functools.partial(_sb_prompt_kernel, tq=tq, kb=kb, hd=hd),
        grid_spec=pltpu.PrefetchScalarGridSpec(
            num_scalar_prefetch=1,
            grid=(b, d // V7X_LANES, s // tq),
            in_specs=[
                qtile,
                pl.BlockSpec((None, V7X_LANES, s), lambda bi, p, i, *_: (bi, p, 0)),
                pl.BlockSpec((None, s, V7X_LANES), lambda bi, p, i, *_: (bi, 0, p)),
                pl.BlockSpec(w.shape, lambda *_: (0, 0)),
            ],
            out_specs=qtile,
        ),
        out_shape=jax.ShapeDtypeStruct((b, s, d), F32),
        compiler_params=_params(3),
        name="sb_prompt",
    )(bias, q, kt, v, w)


def _sb_sample_kernel(pt_ref, q_ref, kn_ref, vn_ref, *refs, n_heads, hd, t_new, n_slots):
    kc_refs, vc_refs = refs[:n_slots], refs[n_slots:2 * n_slots]
    w_ref, b_ref, o_ref, qbd_ref, carry_ref, new_ref, acc_ref = refs[2 * n_slots:]
    j = pl.program_id(1)
    rows = n_heads * t_new
    page = kc_refs[0].shape[-1]
    d = q_ref.shape[-1]
    w = w_ref[...]
    bias = b_ref[...]

    @pl.when(j == 0)
    def _new_keys():
        q = jnp.broadcast_to(q_ref[...][None], (n_heads, t_new, d)).reshape(rows, d)
        r = lax.broadcasted_iota(jnp.int32, (rows, d), 0)
        c = lax.broadcasted_iota(jnp.int32, (rows, d), 1)
        qbd = jnp.where(r // t_new == c // hd, q, 0.0).astype(BF16)
        qbd_ref[...] = qbd
        pad = jnp.zeros((page - t_new, d), F32)
        kn = jnp.concatenate([kn_ref[...], pad], axis=0).astype(BF16)
        vn = jnp.concatenate([vn_ref[...], pad], axis=0).astype(BF16)
        rr = lax.broadcasted_iota(jnp.int32, (rows, page), 0)
        cc = lax.broadcasted_iota(jnp.int32, (rows, page), 1)
        z = _dot_nt(qbd, kn) + bias
        carry, a = _sb_apply(_sb_scores(z, cc < rr % t_new), jnp.zeros((rows, 1), F32), w)
        carry_ref[...] = carry
        new_ref[...] = _dot(a.astype(BF16), vn)
        acc_ref[...] = jnp.zeros(acc_ref.shape, F32)

    qbd = qbd_ref[...]
    scores = [_sb_scores(_dot(qbd, kc_ref[...].astype(BF16)) + bias, None) for kc_ref in kc_refs]
    carry = carry_ref[...]
    acc = acc_ref[...]
    for block_scores, vc_ref in zip(scores, vc_refs):
        carry, a = _sb_apply(block_scores, carry, w)
        acc = acc + _dot(vc_ref[...].astype(BF16), a.T.astype(BF16))
    carry_ref[...] = carry
    acc_ref[...] = acc

    @pl.when(j == pl.num_programs(1) - 1)
    def _emit():
        total = new_ref[...] + acc_ref[...].T
        head_of_col = lax.broadcasted_iota(jnp.int32, (t_new, d), 1) // hd
        out = jnp.zeros((t_new, d), F32)
        for h in range(n_heads):
            out = jnp.where(head_of_col == h, total[h * t_new:(h + 1) * t_new, :], out)
        o_ref[...] = out


def _sb_sample(page_table, q, k_new, v_new, cache_kt, cache_vt, layer, bias_rows, w, *, n_heads, hd):
    db, t_new, d = q.shape
    n_pages = page_table.shape[1]
    page = cache_kt.shape[-1]
    rows = n_heads * t_new
    n_slots = min(SAMPLE_PAGES_PER_STEP, n_pages)
    seq = pl.BlockSpec((None, t_new, d), lambda b, j, pt: (b, 0, 0))

    def cache(slot):
        return pl.BlockSpec(
            (None, None, d, page),
            lambda b, j, pt: (layer, pt[b, n_pages - 1 - (j * n_slots + slot)], 0, 0))

    slots = [cache(u) for u in range(n_slots)]
    return pl.pallas_call(
        functools.partial(_sb_sample_kernel, n_heads=n_heads, hd=hd, t_new=t_new, n_slots=n_slots),
        grid_spec=pltpu.PrefetchScalarGridSpec(
            num_scalar_prefetch=1,
            grid=(db, n_pages // n_slots),
            in_specs=[seq, seq, seq] + slots + slots + [
                pl.BlockSpec(w.shape, lambda *_: (0, 0)),
                pl.BlockSpec(bias_rows.shape, lambda *_: (0, 0)),
            ],
            out_specs=seq,
            scratch_shapes=[
                pltpu.VMEM((rows, d), BF16),
                pltpu.VMEM((rows, 1), F32),
                pltpu.VMEM((rows, d), F32),
                pltpu.VMEM((d, rows), F32),
            ],
        ),
        out_shape=jax.ShapeDtypeStruct((db, t_new, d), F32),
        compiler_params=_params(2),
        name="sb_sample",
    )(page_table, q, k_new, v_new, *[cache_kt] * n_slots, *[cache_vt] * n_slots, w, bias_rows)


def _suffix_sum_matrix(kb):
    j = lax.broadcasted_iota(jnp.int32, (kb, kb), 0)
    s = lax.broadcasted_iota(jnp.int32, (kb, kb), 1)
    return (j > s).astype(BF16)


def kernel(x_prompt, x_sample, cache_k, cache_v, state_pool, state_conv, page_table, norm_mix, norm_mlp, norm_final, w_in_mix, w_out_mix, w_pool_grp, pool_scale, conv_w, conv_b, conv_ln_g, conv_ln_b, w_qkv, w_o, sb_bias, w_up, w_down):
    b, s, d = x_prompt.shape
    db, t_new, _ = x_sample.shape
    depth = norm_mix.shape[0]
    n_sb, n_pool_pages, page, n_heads, hd = cache_k.shape
    past_len = page_table.shape[1] * page
    pool_w = state_pool.shape[-1]
    conv_width = state_conv.shape[-1]
    n_pool_hist = state_pool.shape[2]
    n_conv_hist = state_conv.shape[2]
    q_scale = float(hd) ** -0.5 * LOG2E
    bias2 = sb_bias * LOG2E

    row = lambda x: x.reshape(1, -1)
    hp = x_prompt.reshape(b * s, d)
    hs = x_sample.transpose(1, 0, 2).reshape(t_new * db, d)
    cache_kt = cache_k.transpose(0, 1, 3, 4, 2).reshape(n_sb, n_pool_pages, d, page)
    cache_vt = cache_v.transpose(0, 1, 3, 4, 2).reshape(n_sb, n_pool_pages, d, page)
    state_pool_t = state_pool.transpose(0, 2, 1, 3)
    state_conv_t = state_conv.transpose(0, 2, 1, 3)
    w_sums_prompt = _suffix_sum_matrix(min(ATTN_K_BLOCK, s))
    w_sums_sample = _suffix_sum_matrix(page)
    g_final = row(norm_final)

    kt_p = vt_p = kt_s = vt_s = None
    pool_p, pool_s, conv_p, conv_s = [], [], [], []
    for l in range(depth):
        final = l == depth - 1
        g_mix = row(norm_mix[l])
        mlp_w = (row(norm_mlp[l]), w_up[l].astype(BF16), w_down[l].astype(BF16), g_final, final)
        if l % 2 == 0:
            m = l // 2
            w_in = w_in_mix[m].astype(BF16)
            w_out = w_out_mix[m].astype(BF16)
            core_w = (w_pool_grp[m].astype(BF16), row(pool_scale[m]), conv_w[m], row(conv_b[m]),
                      row(conv_ln_g[m]), row(conv_ln_b[m]))

            ug_p = _mix_in(hp, g_mix, w_in).reshape(b, s, pool_w + conv_width)
            ug_s = _mix_in(hs, g_mix, w_in).reshape(t_new, db, pool_w + conv_width)
            cat_p = _mix_core(ug_p, jnp.zeros((b, n_pool_hist, pool_w), F32),
                              jnp.zeros((b, n_conv_hist, conv_width), F32), *core_w,
                              time_axis=1, nb=1, tt=min(TOKEN_TILE, s), pos0=0)
            cat_s = _mix_core(ug_s, state_pool_t[m], state_conv_t[m], *core_w,
                              time_axis=0, nb=min(SAMPLE_SEQS, db), tt=t_new, pos0=past_len)
            pool_p.append(ug_p[:, s - n_pool_hist:, :pool_w])
            conv_p.append(ug_p[:, s - n_conv_hist:, pool_w:])
            pool_s.append(jnp.concatenate([state_pool_t[m], ug_s[:, :, :pool_w]], axis=0)[-n_pool_hist:])
            conv_s.append(jnp.concatenate([state_conv_t[m], ug_s[:, :, pool_w:]], axis=0)[-n_conv_hist:])
            hp = _post(hp, cat_p.reshape(b * s, -1), w_out, *mlp_w)
            hs = _post(hs, cat_s.reshape(t_new * db, -1), w_out, *mlp_w)
        else:
            a = l // 2
            wq, wk, wv = (w_qkv[a][:, i * d:(i + 1) * d].astype(BF16) for i in range(3))
            qkv_w = (wq, wk, wv, wk.T, wv.T, q_scale)
            wo = w_o[a].astype(BF16)

            q, _, vb, ktb, kt_p, vt_p = _qkv(hp, g_mix, *qkv_w, kt_p, vt_p, a, n_sb, b)
            o_p = _sb_prompt(q.reshape(b, s, d), ktb, vb.reshape(b, s, d), bias2[a], w_sums_prompt, hd=hd)
            hp = _post(hp, o_p.reshape(b * s, d), wo, *mlp_w)

            q, kb, vb, _, kt_s, vt_s = _qkv(hs, g_mix, *qkv_w, kt_s, vt_s, a, n_sb, t_new)
            by_seq = lambda x: x.reshape(t_new, db, d).transpose(1, 0, 2).astype(F32)
            bias_rows = jnp.broadcast_to(jnp.repeat(bias2[a], t_new)[:, None], (n_heads * t_new, page))
            o_s = _sb_sample(page_table, by_seq(q), by_seq(kb), by_seq(vb), cache_kt, cache_vt, a,
                             bias_rows, w_sums_sample, n_heads=n_heads, hd=hd)
            hs = _post(hs, o_s.transpose(1, 0, 2).reshape(t_new * db, d), wo, *mlp_w)

    kv_prompt = lambda x: x.reshape(n_sb, b, n_heads, hd, s).transpose(0, 1, 4, 2, 3)
    kv_sample = lambda x: x.reshape(n_sb, t_new, n_heads, hd, db).transpose(0, 4, 1, 2, 3)
    state_sample = lambda xs: jnp.stack(xs).transpose(0, 2, 1, 3)
    return (hp.reshape(b, s, d), hs.reshape(t_new, db, d).transpose(1, 0, 2),
            kv_prompt(kt_p), kv_prompt(vt_p), kv_sample(kt_s), kv_sample(vt_s),
            jnp.stack(pool_p), state_sample(pool_s), jnp.stack(conv_p), state_sample(conv_s))
```

```python
import functools

import jax
import jax.numpy as jnp
from jax import lax
from jax.experimental import pallas as pl
from jax.experimental.pallas import tpu as pltpu

F32 = jnp.float32
BF16 = jnp.bfloat16

EPS = 1e-6
POOL_WINDOWS = (2, 4, 8, 16)

V7X_SUBLANES = 8
V7X_LANES = 128
V7X_VMEM_BYTES = 64 * 1024 * 1024
VMEM_LIMIT_BYTES = V7X_VMEM_BYTES * 7 // 8

TOKEN_TILE = 512
MLP_CHUNK = 1024
MIX_CHUNK_VREGS = 16
ATTN_K_BLOCK = 256
SAMPLE_PAGES_PER_STEP = 8
LOG2E = 1.4426950408889634
POOL_HALO = 16
SAMPLE_SEQS = 32


def _params(n_axes):
    return pltpu.CompilerParams(
        dimension_semantics=("arbitrary",) * n_axes,
        vmem_limit_bytes=VMEM_LIMIT_BYTES,
    )


def _resident(shape):
    zeros = (0,) * len(shape)
    return pl.BlockSpec(shape, lambda *_: zeros, pipeline_mode=pl.Buffered(1))


def _rmsnorm(x, g):
    return x * lax.rsqrt(jnp.mean(x * x, axis=-1, keepdims=True) + EPS) * g


def _sigmoid(x):
    return 1.0 / (1.0 + jnp.exp(-x))


def _dot(a, b):
    return jnp.dot(a, b, preferred_element_type=F32)


def _dot_nt(a, b):
    return lax.dot_general(a, b, (((1,), (1,)), ((), ())), preferred_element_type=F32)


def _mix_in_kernel(x_ref, g_ref, w_ref, o_ref, *, pool_w, conv_w):
    xn = _rmsnorm(x_ref[...], g_ref[...]).astype(BF16)
    o_ref[:, :pool_w] = _dot(xn, w_ref[:, :pool_w])
    a = _dot(xn, w_ref[:, pool_w:pool_w + conv_w])
    gate = _dot(xn, w_ref[:, pool_w + conv_w:])
    o_ref[:, pool_w:] = a * _sigmoid(gate)


def _mix_in(h, g, w):
    n, d = h.shape
    cols = w.shape[1]
    pool_w = conv_w = cols // 3
    tm = min(TOKEN_TILE, n)
    return pl.pallas_call(
        functools.partial(_mix_in_kernel, pool_w=pool_w, conv_w=conv_w),
        grid=(n // tm,),
        in_specs=[
            pl.BlockSpec((tm, d), lambda i: (i, 0)),
            _resident((1, d)),
            _resident((d, cols)),
        ],
        out_specs=pl.BlockSpec((tm, pool_w + conv_w), lambda i: (i, 0)),
        out_shape=jax.ShapeDtypeStruct((n, pool_w + conv_w), F32),
        compiler_params=_params(1),
        name="mix_in",
    )(h, g, w)


def _qkv_kernel(x_ref, g_ref, wq_ref, wk_ref, wv_ref, wkt_ref, wvt_ref, *refs, q_scale, has_alias):
    q_ref, kb_ref, vb_ref, ktb_ref, kt_ref, vt_ref = refs[2 if has_alias else 0:]
    xn = _rmsnorm(x_ref[...], g_ref[...]).astype(BF16)
    q_ref[...] = (_dot(xn, wq_ref[...]) * q_scale).astype(BF16)
    kb_ref[...] = _dot(xn, wk_ref[...]).astype(BF16)
    vb_ref[...] = _dot(xn, wv_ref[...]).astype(BF16)
    kt = _dot_nt(wkt_ref[...], xn)
    kt_ref[...] = kt
    ktb_ref[...] = kt.astype(BF16)
    vt_ref[...] = _dot_nt(wvt_ref[...], xn)


def _qkv(h, g, wq, wk, wv, wkt, wvt, q_scale, kt_all, vt_all, layer, n_layers, groups):
    n, d = h.shape
    glen = n // groups
    tm = min(TOKEN_TILE, glen)
    per_group = glen // tm
    tile = pl.BlockSpec((tm, d), lambda i: (i, 0))
    t_tile = pl.BlockSpec((None, None, d, tm), lambda i: (layer, i // per_group, 0, i % per_group))
    tb_tile = pl.BlockSpec((None, d, tm), lambda i: (i // per_group, 0, i % per_group))
    weights = [_resident((d, d))] * 5
    has_alias = kt_all is not None
    alias_args = (kt_all, vt_all) if has_alias else ()
    alias_specs = [pl.BlockSpec(memory_space=pl.ANY)] * len(alias_args)
    t_shape = jax.ShapeDtypeStruct((n_layers, groups, d, glen), F32)
    return pl.pallas_call(
        functools.partial(_qkv_kernel, q_scale=q_scale, has_alias=has_alias),
        grid=(n // tm,),
        in_specs=[tile, _resident((1, d))] + weights + alias_specs,
        out_specs=[tile, tile, tile, tb_tile, t_tile, t_tile],
        out_shape=[jax.ShapeDtypeStruct((n, d), BF16)] * 3
        + [jax.ShapeDtypeStruct((groups, d, glen), BF16), t_shape, t_shape],
        input_output_aliases={7: 4, 8: 5} if has_alias else {},
        compiler_params=_params(1),
        name="qkv",
    )(h, g, wq, wk, wv, wkt, wvt, *alias_args)


def _post_kernel(h_ref, a_ref, wo_ref, g_ref, wup_ref, wdn_ref, gf_ref, o_ref, *, n_chunks, final):
    h1 = h_ref[...] + _dot(a_ref[...].astype(BF16), wo_ref[...])
    xn = _rmsnorm(h1, g_ref[...]).astype(BF16)
    ck = wup_ref.shape[1] // n_chunks
    acc = h1
    for c in range(n_chunks):
        up = _dot(xn, wup_ref[:, c * ck:(c + 1) * ck])
        act = jnp.square(jnp.maximum(up, 0.0)).astype(BF16)
        acc = acc + _dot(act, wdn_ref[c * ck:(c + 1) * ck, :])
    if final:
        acc = _rmsnorm(acc, gf_ref[...])
    o_ref[...] = acc


def _post(h, a, wo, g, wup, wdn, g_final, final):
    n, d = h.shape
    dff = wup.shape[1]
    tm = min(TOKEN_TILE, n)
    tile = pl.BlockSpec((tm, d), lambda i: (i, 0))
    return pl.pallas_call(
        functools.partial(_post_kernel, n_chunks=max(1, dff // MLP_CHUNK), final=final),
        grid=(n // tm,),
        in_specs=[
            tile,
            pl.BlockSpec((tm, a.shape[1]), lambda i: (i, 0)),
            _resident(wo.shape),
            _resident((1, d)),
            _resident((d, dff)),
            _resident((dff, d)),
            _resident((1, d)),
        ],
        out_specs=tile,
        out_shape=jax.ShapeDtypeStruct((n, d), F32),
        compiler_params=_params(1),
        name="post_mlp",
    )(h, a, wo, g, wup, wdn, g_final)


def _mix_core_kernel(ug_ref, hp_ref, hc_ref, wg_ref, sc_ref, cw_ref, cb_ref, lg_ref, lb_ref,
                     o_ref, uf_ref, gf_ref, d_ref, *gs_refs, time_axis, nb, tt, pos0, n_t, conv_halo):
    ti = pl.program_id(1)
    pool_w = hp_ref.shape[-1]
    conv_w = hc_ref.shape[-1]
    n_pool_hist = hp_ref.shape[time_axis]
    n_conv_hist = hc_ref.shape[time_axis]
    taps = cw_ref.shape[0]
    group_w = pool_w // len(POOL_WINDOWS)

    def rows(start, size, lanes=slice(None)):
        t = slice(start, start + size)
        return (t, slice(None), lanes) if time_axis == 0 else (slice(None), t, lanes)

    def shape(n_rows, width):
        return (n_rows, nb, width) if time_axis == 0 else (nb, n_rows, width)

    if time_axis == 0:
        rc = max(1, MIX_CHUNK_VREGS * V7X_SUBLANES * V7X_LANES // (nb * conv_w))
    else:
        rc = MIX_CHUNK_VREGS * V7X_SUBLANES * V7X_LANES // conv_w
    rc = min(rc, tt)
    row_iota_shape = (rc, 1, 1) if time_axis == 0 else (1, rc, 1)

    @pl.when(ti == 0)
    def _load_history():
        uf_ref[rows(0, POOL_HALO - n_pool_hist)] = jnp.zeros(shape(POOL_HALO - n_pool_hist, pool_w), F32)
        uf_ref[rows(POOL_HALO - n_pool_hist, n_pool_hist)] = hp_ref[...]
        gf_ref[rows(0, conv_halo - n_conv_hist)] = jnp.zeros(shape(conv_halo - n_conv_hist, conv_w), F32)
        gf_ref[rows(conv_halo - n_conv_hist, n_conv_hist)] = hc_ref[...]

    uf_ref[rows(POOL_HALO, tt)] = ug_ref[:, :, :pool_w]
    gf_ref[rows(conv_halo, tt)] = ug_ref[:, :, pool_w:]

    if time_axis == 1:
        gs_ref, = gs_refs
        n_rows = conv_halo + tt
        for sft in range(1, V7X_SUBLANES):
            gs_ref[sft - 1, :, 0:n_rows - sft, :] = gf_ref[:, sft:n_rows, :]

    def conv_rows(start):
        sft = start % V7X_SUBLANES
        if time_axis == 0 or sft == 0:
            return gf_ref[rows(start, rc)]
        return gs_ref[sft - 1, :, start - sft:start - sft + rc, :]

    cb = cb_ref[...].reshape(1, 1, conv_w)
    lg = lg_ref[...].reshape(1, 1, conv_w)
    lb = lb_ref[...].reshape(1, 1, conv_w)

    for r0 in range(0, tt, rc):
        pos = pos0 + ti * tt + r0 + lax.broadcasted_iota(jnp.int32, row_iota_shape, time_axis)
        for gi, win in enumerate(POOL_WINDOWS):
            lanes = slice(gi * group_w, (gi + 1) * group_w)
            u_new = uf_ref[rows(POOL_HALO + r0, rc, lanes)]
            s = u_new
            for i in range(1, win):
                s = s + uf_ref[rows(POOL_HALO + r0 - i, rc, lanes)]
            cnt = jnp.minimum(win, pos + 1).astype(F32)
            d_ref[rows(r0, rc, lanes)] = s / cnt - u_new

        base = conv_halo - n_conv_hist + r0
        acc = jnp.zeros(shape(rc, conv_w), F32)
        for k in range(taps):
            acc = acc + conv_rows(base + k) * cw_ref[k:k + 1, :].reshape(1, 1, conv_w)
        y = acc + cb
        mu = jnp.mean(y, axis=-1, keepdims=True)
        yc = y - mu
        var = jnp.mean(yc * yc, axis=-1, keepdims=True)
        yn = yc * lax.rsqrt(var + EPS) * lg + lb
        o_ref[rows(r0, rc, slice(pool_w, pool_w + conv_w))] = yn * _sigmoid(yn)

    for gi in range(len(POOL_WINDOWS)):
        lanes = slice(gi * group_w, (gi + 1) * group_w)
        d = d_ref[:, :, lanes].reshape(nb * tt, group_w).astype(BF16)
        y = _dot(d, wg_ref[gi]) * sc_ref[:, lanes]
        o_ref[:, :, lanes] = y.reshape(shape(tt, group_w))

    if n_t > 1:
        uf_ref[rows(0, POOL_HALO)] = uf_ref[rows(tt, POOL_HALO)]
        gf_ref[rows(0, conv_halo)] = gf_ref[rows(tt, conv_halo)]


def _mix_core(ug, hist_pool, hist_conv, w_grp, scale, conv_w, conv_b, ln_g, ln_b, *,
              time_axis, nb, tt, pos0):
    seq_axis = 1 - time_axis
    n_seq, t, width = ug.shape[seq_axis], ug.shape[time_axis], ug.shape[2]
    pool_w = hist_pool.shape[-1]
    cw = hist_conv.shape[-1]
    n_t = t // tt
    conv_halo = -(-hist_conv.shape[time_axis] // V7X_SUBLANES) * V7X_SUBLANES

    def block(n_rows, c):
        if time_axis == 0:
            return pl.BlockSpec((n_rows, nb, c), lambda b, i: (i if n_rows == tt else 0, b, 0))
        return pl.BlockSpec((nb, n_rows, c), lambda b, i: (b, i if n_rows == tt else 0, 0))

    def scratch(n_rows, c):
        return pltpu.VMEM((n_rows, nb, c) if time_axis == 0 else (nb, n_rows, c), F32)

    return pl.pallas_call(
        functools.partial(_mix_core_kernel, time_axis=time_axis, nb=nb, tt=tt, pos0=pos0, n_t=n_t,
                          conv_halo=conv_halo),
        grid=(n_seq // nb, n_t),
        in_specs=[
            block(tt, width),
            block(hist_pool.shape[time_axis], pool_w),
            block(hist_conv.shape[time_axis], cw),
            _resident(w_grp.shape),
            _resident(scale.shape),
            _resident(conv_w.shape),
            _resident(conv_b.shape),
            _resident(ln_g.shape),
            _resident(ln_b.shape),
        ],
        out_specs=block(tt, width),
        out_shape=jax.ShapeDtypeStruct(ug.shape, F32),
        scratch_shapes=[scratch(POOL_HALO + tt, pool_w), scratch(conv_halo + tt, cw), scratch(tt, pool_w)]
        + ([pltpu.VMEM((V7X_SUBLANES - 1, nb, conv_halo + tt, cw), F32)] if time_axis == 1 else []),
        compiler_params=_params(2),
        name="mix_core",
    )(ug, hist_pool, hist_conv, w_grp, scale, conv_w, conv_b, ln_g, ln_b)


def _sb_scores(z, mask):
    sign = jnp.uint32(0x80000000)
    neg_abs = lax.bitcast_convert_type(lax.bitcast_convert_type(z, jnp.uint32) | sign, F32)
    sp = jnp.maximum(z, 0.0) + jnp.log(1.0 + jnp.exp2(neg_abs)) * LOG2E
    log_beta = z - sp
    if mask is not None:
        sp = jnp.where(mask, sp, 0.0)
        log_beta = jnp.where(mask, log_beta, -jnp.inf)
    return sp.astype(BF16), log_beta, sp[:, :1]


def _sb_apply(scores, carry, w, first=0):
    sp, log_beta, sp_first = scores
    later = _dot(sp, w) + carry
    return later[:, first:first + 1] + sp_first, jnp.exp2(log_beta - later)


def _sb_prompt_kernel(bias_ref, q_ref, kt_ref, v_ref, w_ref, o_ref, sp_ref, lb_ref, carry_ref, acc_ref,
                      *, tq, kb, hd):
    col_block = pl.program_id(1)
    qi = pl.program_id(2)
    width = q_ref.shape[-1]
    heads = width // hd
    lane = lax.broadcasted_iota(jnp.int32, (tq, width), 1)
    row = lax.broadcasted_iota(jnp.int32, (tq, kb), 0)
    col = lax.broadcasted_iota(jnp.int32, (tq, kb), 1)
    q = q_ref[...].astype(F32)
    w = w_ref[...]
    in_head = [(lane >= hh * hd) & (lane < (hh + 1) * hd) for hh in range(heads)]
    qh = [jnp.where(m, q, 0.0).astype(BF16) for m in in_head]
    bias = [bias_ref[col_block * heads + hh] for hh in range(heads)]

    def score(slot, blk, mask):
        kt = kt_ref[:, pl.ds(pl.multiple_of(blk * kb, kb), kb)]
        for hh in range(heads):
            sp, log_beta, _ = _sb_scores(_dot(qh[hh], kt) + bias[hh], mask)
            sp_ref[slot, hh] = sp
            lb_ref[slot, hh] = log_beta

    def accumulate(slot, blk):
        v = v_ref[pl.ds(pl.multiple_of(blk * kb, kb), kb), :]
        for hh in range(heads):
            sp = sp_ref[slot, hh]
            scores = (sp, lb_ref[slot, hh], sp[:, :1].astype(F32))
            carry_ref[hh], a = _sb_apply(scores, carry_ref[hh], w)
            acc_ref[hh] += _dot(a.astype(BF16), v)

    first = 2 * qi
    carry_ref[...] = jnp.zeros(carry_ref.shape, F32)
    acc_ref[...] = jnp.zeros(acc_ref.shape, F32)
    score(0, first + 1, col + kb < row)
    score(1, first, col < row)
    accumulate(0, first + 1)

    @pl.loop(0, qi)
    def _(i):
        blk = first - 1 - 2 * i
        score(0, blk, None)
        accumulate(1, blk + 1)
        score(1, blk - 1, None)
        accumulate(0, blk)

    accumulate(1, 0)
    out = acc_ref[0]
    for hh in range(1, heads):
        out = jnp.where(in_head[hh], acc_ref[hh], out)
    o_ref[...] = out


def _sb_prompt(q, kt, v, bias, w, *, hd):
    b, s, d = q.shape
    kb = w.shape[0]
    tq = 2 * kb
    heads = V7X_LANES // hd
    qtile = pl.BlockSpec((None, tq, V7X_LANES), lambda bi, p, i, *_: (bi, i, p))
    return pl.pallas_call(
        functools.partial(_sb_prompt_kernel, tq=tq, kb=kb, hd=hd),
        grid_spec=pltpu.PrefetchScalarGridSpec(
            num_scalar_prefetch=1,
            grid=(b, d // V7X_LANES, s // tq),
            in_specs=[
                qtile,
                pl.BlockSpec((None, V7X_LANES, s), lambda bi, p, i, *_: (bi, p, 0)),
                pl.BlockSpec((None, s, V7X_LANES), lambda bi, p, i, *_: (bi, 0, p)),
                _resident(w.shape),
            ],
            out_specs=qtile,
            scratch_shapes=[
                pltpu.VMEM((2, heads, tq, kb), BF16),
                pltpu.VMEM((2, heads, tq, kb), F32),
                pltpu.VMEM((heads, tq, 1), F32),
                pltpu.VMEM((heads, tq, V7X_LANES), F32),
            ],
        ),
        out_shape=jax.ShapeDtypeStruct((b, s, d), F32),
        compiler_params=_params(3),
        name="sb_prompt",
    )(bias, q, kt, v, w)


def _sb_sample_kernel(pt_ref, q_ref, kn_ref, vn_ref, *refs, n_heads, hd, t_new, n_slots):
    kc_refs, vc_refs = refs[:n_slots], refs[n_slots:2 * n_slots]
    w_ref, wall_ref, b_ref, o_ref, qbd_ref, carry_ref, new_ref, acc_ref = refs[2 * n_slots:]
    j = pl.program_id(1)
    rows = n_heads * t_new
    page = kc_refs[0].shape[-1]
    d = q_ref.shape[-1]
    w = w_ref[...]
    bias = b_ref[...]

    @pl.when(j == 0)
    def _new_keys():
        q = jnp.broadcast_to(q_ref[...][None], (n_heads, t_new, d)).reshape(rows, d)
        r = lax.broadcasted_iota(jnp.int32, (rows, d), 0)
        c = lax.broadcasted_iota(jnp.int32, (rows, d), 1)
        qbd = jnp.where(r // t_new == c // hd, q, 0.0).astype(BF16)
        qbd_ref[...] = qbd
        pad = jnp.zeros((page - t_new, d), F32)
        kn = jnp.concatenate([kn_ref[...], pad], axis=0).astype(BF16)
        vn = jnp.concatenate([vn_ref[...], pad], axis=0).astype(BF16)
        rr = lax.broadcasted_iota(jnp.int32, (rows, page), 0)
        cc = lax.broadcasted_iota(jnp.int32, (rows, page), 1)
        z = _dot_nt(qbd, kn) + bias
        carry, a = _sb_apply(_sb_scores(z, cc < rr % t_new), jnp.zeros((rows, 1), F32), w)
        carry_ref[...] = carry
        new_ref[...] = _dot(a.astype(BF16), vn)
        acc_ref[...] = jnp.zeros(acc_ref.shape, F32)

    kt = jnp.concatenate([kc_ref[...].astype(BF16) for kc_ref in kc_refs], axis=1)
    vt = jnp.concatenate([vc_ref[...].astype(BF16) for vc_ref in vc_refs], axis=1)
    z = _dot(qbd_ref[...], kt) + jnp.concatenate([bias] * n_slots, axis=1)
    sp, log_beta, _ = _sb_scores(z, None)
    oldest = (n_slots - 1) * page
    scores = (sp, log_beta, sp[:, oldest:oldest + 1].astype(F32))
    carry, a = _sb_apply(scores, carry_ref[...], wall_ref[...], first=oldest)
    carry_ref[...] = carry
    acc_ref[...] += _dot(vt, a.T.astype(BF16))

    @pl.when(j == pl.num_programs(1) - 1)
    def _emit():
        total = new_ref[...] + acc_ref[...].T
        head_of_col = lax.broadcasted_iota(jnp.int32, (t_new, d), 1) // hd
        out = jnp.zeros((t_new, d), F32)
        for h in range(n_heads):
            out = jnp.where(head_of_col == h, total[h * t_new:(h + 1) * t_new, :], out)
        o_ref[...] = out


def _sb_sample(page_table, q, k_new, v_new, cache_kt, cache_vt, layer, bias_rows, w, *, n_heads, hd):
    db, t_new, d = q.shape
    n_pages = page_table.shape[1]
    page = cache_kt.shape[-1]
    rows = n_heads * t_new
    n_slots = min(SAMPLE_PAGES_PER_STEP, n_pages)
    seq = pl.BlockSpec((None, t_new, d), lambda b, j, pt: (b, 0, 0))

    def cache(slot):
        return pl.BlockSpec(
            (None, None, d, page),
            lambda b, j, pt: (layer, pt[b, n_pages - 1 - (j * n_slots + slot)], 0, 0))

    slots = [cache(u) for u in range(n_slots)]
    kj = lax.broadcasted_iota(jnp.int32, (n_slots * page, n_slots * page), 0)
    ks = lax.broadcasted_iota(jnp.int32, (n_slots * page, n_slots * page), 1)
    w_step = ((kj // page < ks // page) | ((kj // page == ks // page) & (kj > ks))).astype(BF16)
    return pl.pallas_call(
        functools.partial(_sb_sample_kernel, n_heads=n_heads, hd=hd, t_new=t_new, n_slots=n_slots),
        grid_spec=pltpu.PrefetchScalarGridSpec(
            num_scalar_prefetch=1,
            grid=(db, n_pages // n_slots),
            in_specs=[seq, seq, seq] + slots + slots + [
                _resident(w.shape), _resident(w_step.shape), _resident(bias_rows.shape),
            ],
            out_specs=seq,
            scratch_shapes=[
                pltpu.VMEM((rows, d), BF16),
                pltpu.VMEM((rows, 1), F32),
                pltpu.VMEM((rows, d), F32),
                pltpu.VMEM((d, rows), F32),
            ],
        ),
        out_shape=jax.ShapeDtypeStruct((db, t_new, d), F32),
        compiler_params=_params(2),
        name="sb_sample",
    )(page_table, q, k_new, v_new, *[cache_kt] * n_slots, *[cache_vt] * n_slots, w, w_step, bias_rows)


def _suffix_sum_matrix(kb):
    j = lax.broadcasted_iota(jnp.int32, (kb, kb), 0)
    s = lax.broadcasted_iota(jnp.int32, (kb, kb), 1)
    return (j > s).astype(BF16)


def kernel(x_prompt, x_sample, cache_k, cache_v, state_pool, state_conv, page_table, norm_mix, norm_mlp, norm_final, w_in_mix, w_out_mix, w_pool_grp, pool_scale, conv_w, conv_b, conv_ln_g, conv_ln_b, w_qkv, w_o, sb_bias, w_up, w_down):
    b, s, d = x_prompt.shape
    db, t_new, _ = x_sample.shape
    depth = norm_mix.shape[0]
    n_sb, n_pool_pages, page, n_heads, hd = cache_k.shape
    past_len = page_table.shape[1] * page
    pool_w = state_pool.shape[-1]
    conv_width = state_conv.shape[-1]
    n_pool_hist = state_pool.shape[2]
    n_conv_hist = state_conv.shape[2]
    q_scale = float(hd) ** -0.5 * LOG2E
    bias2 = sb_bias * LOG2E

    row = lambda x: x.reshape(1, -1)
    hp = x_prompt.reshape(b * s, d)
    hs = x_sample.transpose(1, 0, 2).reshape(t_new * db, d)
    cache_kt = cache_k.transpose(0, 1, 3, 4, 2).reshape(n_sb, n_pool_pages, d, page)
    cache_vt = cache_v.transpose(0, 1, 3, 4, 2).reshape(n_sb, n_pool_pages, d, page)
    state_pool_t = state_pool.transpose(0, 2, 1, 3)
    state_conv_t = state_conv.transpose(0, 2, 1, 3)
    w_sums_prompt = _suffix_sum_matrix(min(ATTN_K_BLOCK, s // 2))
    w_sums_sample = _suffix_sum_matrix(page)
    g_final = row(norm_final)

    kt_p = vt_p = kt_s = vt_s = None
    pool_p, pool_s, conv_p, conv_s = [], [], [], []
    for l in range(depth):
        final = l == depth - 1
        g_mix = row(norm_mix[l])
        mlp_w = (row(norm_mlp[l]), w_up[l].astype(BF16), w_down[l].astype(BF16), g_final, final)
        if l % 2 == 0:
            m = l // 2
            w_in = w_in_mix[m].astype(BF16)
            w_out = w_out_mix[m].astype(BF16)
            core_w = (w_pool_grp[m].astype(BF16), row(pool_scale[m]), conv_w[m], row(conv_b[m]),
                      row(conv_ln_g[m]), row(conv_ln_b[m]))

            ug_p = _mix_in(hp, g_mix, w_in).reshape(b, s, pool_w + conv_width)
            ug_s = _mix_in(hs, g_mix, w_in).reshape(t_new, db, pool_w + conv_width)
            cat_p = _mix_core(ug_p, jnp.zeros((b, n_pool_hist, pool_w), F32),
                              jnp.zeros((b, n_conv_hist, conv_width), F32), *core_w,
                              time_axis=1, nb=1, tt=min(TOKEN_TILE, s), pos0=0)
            cat_s = _mix_core(ug_s, state_pool_t[m], state_conv_t[m], *core_w,
                              time_axis=0, nb=min(SAMPLE_SEQS, db), tt=t_new, pos0=past_len)
            pool_p.append(ug_p[:, s - n_pool_hist:, :pool_w])
            conv_p.append(ug_p[:, s - n_conv_hist:, pool_w:])
            pool_s.append(jnp.concatenate([state_pool_t[m], ug_s[:, :, :pool_w]], axis=0)[-n_pool_hist:])
            conv_s.append(jnp.concatenate([state_conv_t[m], ug_s[:, :, pool_w:]], axis=0)[-n_conv_hist:])
            hp = _post(hp, cat_p.reshape(b * s, -1), w_out, *mlp_w)
            hs = _post(hs, cat_s.reshape(t_new * db, -1), w_out, *mlp_w)
        else:
            a = l // 2
            wq, wk, wv = (w_qkv[a][:, i * d:(i + 1) * d].astype(BF16) for i in range(3))
            qkv_w = (wq, wk, wv, wk.T, wv.T, q_scale)
            wo = w_o[a].astype(BF16)

            q, _, vb, ktb, kt_p, vt_p = _qkv(hp, g_mix, *qkv_w, kt_p, vt_p, a, n_sb, b)
            o_p = _sb_prompt(q.reshape(b, s, d), ktb, vb.reshape(b, s, d), bias2[a], w_sums_prompt, hd=hd)
            hp = _post(hp, o_p.reshape(b * s, d), wo, *mlp_w)

            q, kb, vb, _, kt_s, vt_s = _qkv(hs, g_mix, *qkv_w, kt_s, vt_s, a, n_sb, t_new)
            by_seq = lambda x: x.reshape(t_new, db, d).transpose(1, 0, 2).astype(F32)
            bias_rows = jnp.broadcast_to(jnp.repeat(bias2[a], t_new)[:, None], (n_heads * t_new, page))
            o_s = _sb_sample(page_table, by_seq(q), by_seq(kb), by_seq(vb), cache_kt, cache_vt, a,
                             bias_rows, w_sums_sample, n_heads=n_heads, hd=hd)
            hs = _post(hs, o_s.transpose(1, 0, 2).reshape(t_new * db, d), wo, *mlp_w)

    kv_prompt = lambda x: x.reshape(n_sb, b, n_heads, hd, s).transpose(0, 1, 4, 2, 3)
    kv_sample = lambda x: x.reshape(n_sb, t_new, n_heads, hd, db).transpose(0, 4, 1, 2, 3)
    state_sample = lambda xs: jnp.stack(xs).transpose(0, 2, 1, 3)
    return (hp.reshape(b, s, d), hs.reshape(t_new, db, d).transpose(1, 0, 2),
            kv_prompt(kt_p), kv_prompt(vt_p), kv_sample(kt_s), kv_sample(vt_s),
            jnp.stack(pool_p), state_sample(pool_s), jnp.stack(conv_p), state_sample(conv_s))
```

```python
import functools

import jax
import jax.numpy as jnp
from jax import lax
from jax.experimental import pallas as pl
from jax.experimental.pallas import tpu as pltpu

F32 = jnp.float32
BF16 = jnp.bfloat16

EPS = 1e-6
POOL_WINDOWS = (2, 4, 8, 16)

V7X_SUBLANES = 8
V7X_LANES = 128
V7X_VMEM_BYTES = 64 * 1024 * 1024
VMEM_LIMIT_BYTES = V7X_VMEM_BYTES * 7 // 8

TOKEN_TILE = 512
MLP_CHUNK = 1024
MIX_CHUNK_VREGS = 16
ATTN_K_BLOCK = 256
ATTN_COL_BLOCKS = 2
SAMPLE_PAGES_PER_STEP = 8
LOG2E = 1.4426950408889634
POOL_HALO = 16
SAMPLE_SEQS = 32


def _params(n_axes):
    return pltpu.CompilerParams(
        dimension_semantics=("arbitrary",) * n_axes,
        vmem_limit_bytes=VMEM_LIMIT_BYTES,
    )


def _resident(shape):
    zeros = (0,) * len(shape)
    return pl.BlockSpec(shape, lambda *_: zeros, pipeline_mode=pl.Buffered(1))


def _rmsnorm(x, g):
    return x * lax.rsqrt(jnp.mean(x * x, axis=-1, keepdims=True) + EPS) * g


def _sigmoid(x):
    return 1.0 / (1.0 + jnp.exp(-x))


def _dot(a, b):
    return jnp.dot(a, b, preferred_element_type=F32)


def _dot_nt(a, b):
    return lax.dot_general(a, b, (((1,), (1,)), ((), ())), preferred_element_type=F32)


def _mix_in_kernel(x_ref, g_ref, w_ref, o_ref, *, pool_w, conv_w):
    xn = _rmsnorm(x_ref[...], g_ref[...]).astype(BF16)
    o_ref[:, :pool_w] = _dot(xn, w_ref[:, :pool_w])
    a = _dot(xn, w_ref[:, pool_w:pool_w + conv_w])
    gate = _dot(xn, w_ref[:, pool_w + conv_w:])
    o_ref[:, pool_w:] = a * _sigmoid(gate)


def _mix_in(h, g, w):
    n, d = h.shape
    cols = w.shape[1]
    pool_w = conv_w = cols // 3
    tm = min(TOKEN_TILE, n)
    return pl.pallas_call(
        functools.partial(_mix_in_kernel, pool_w=pool_w, conv_w=conv_w),
        grid=(n // tm,),
        in_specs=[
            pl.BlockSpec((tm, d), lambda i: (i, 0)),
            _resident((1, d)),
            _resident((d, cols)),
        ],
        out_specs=pl.BlockSpec((tm, pool_w + conv_w), lambda i: (i, 0)),
        out_shape=jax.ShapeDtypeStruct((n, pool_w + conv_w), F32),
        compiler_params=_params(1),
        name="mix_in",
    )(h, g, w)


def _qkv_kernel(x_ref, g_ref, wq_ref, wk_ref, wv_ref, wkt_ref, wvt_ref, *refs, q_scale, has_alias):
    q_ref, kb_ref, vb_ref, ktb_ref, kt_ref, vt_ref = refs[2 if has_alias else 0:]
    xn = _rmsnorm(x_ref[...], g_ref[...]).astype(BF16)
    q_ref[...] = (_dot(xn, wq_ref[...]) * q_scale).astype(BF16)
    kb_ref[...] = _dot(xn, wk_ref[...]).astype(BF16)
    vb_ref[...] = _dot(xn, wv_ref[...]).astype(BF16)
    kt = _dot_nt(wkt_ref[...], xn)
    kt_ref[...] = kt
    ktb_ref[...] = kt.astype(BF16)
    vt_ref[...] = _dot_nt(wvt_ref[...], xn)


def _qkv(h, g, wq, wk, wv, wkt, wvt, q_scale, kt_all, vt_all, layer, n_layers, groups):
    n, d = h.shape
    glen = n // groups
    tm = min(TOKEN_TILE, glen)
    per_group = glen // tm
    tile = pl.BlockSpec((tm, d), lambda i: (i, 0))
    t_tile = pl.BlockSpec((None, None, d, tm), lambda i: (layer, i // per_group, 0, i % per_group))
    tb_tile = pl.BlockSpec((None, d, tm), lambda i: (i // per_group, 0, i % per_group))
    weights = [_resident((d, d))] * 5
    has_alias = kt_all is not None
    alias_args = (kt_all, vt_all) if has_alias else ()
    alias_specs = [pl.BlockSpec(memory_space=pl.ANY)] * len(alias_args)
    t_shape = jax.ShapeDtypeStruct((n_layers, groups, d, glen), F32)
    return pl.pallas_call(
        functools.partial(_qkv_kernel, q_scale=q_scale, has_alias=has_alias),
        grid=(n // tm,),
        in_specs=[tile, _resident((1, d))] + weights + alias_specs,
        out_specs=[tile, tile, tile, tb_tile, t_tile, t_tile],
        out_shape=[jax.ShapeDtypeStruct((n, d), BF16)] * 3
        + [jax.ShapeDtypeStruct((groups, d, glen), BF16), t_shape, t_shape],
        input_output_aliases={7: 4, 8: 5} if has_alias else {},
        compiler_params=_params(1),
        name="qkv",
    )(h, g, wq, wk, wv, wkt, wvt, *alias_args)


def _post_kernel(h_ref, a_ref, wo_ref, g_ref, wup_ref, wdn_ref, gf_ref, o_ref, *, n_chunks, final):
    h1 = h_ref[...] + _dot(a_ref[...].astype(BF16), wo_ref[...])
    xn = _rmsnorm(h1, g_ref[...]).astype(BF16)
    ck = wup_ref.shape[1] // n_chunks
    acc = h1
    for c in range(n_chunks):
        up = _dot(xn, wup_ref[:, c * ck:(c + 1) * ck])
        act = jnp.square(jnp.maximum(up, 0.0)).astype(BF16)
        acc = acc + _dot(act, wdn_ref[c * ck:(c + 1) * ck, :])
    if final:
        acc = _rmsnorm(acc, gf_ref[...])
    o_ref[...] = acc


def _post(h, a, wo, g, wup, wdn, g_final, final):
    n, d = h.shape
    dff = wup.shape[1]
    tm = min(TOKEN_TILE, n)
    tile = pl.BlockSpec((tm, d), lambda i: (i, 0))
    return pl.pallas_call(
        functools.partial(_post_kernel, n_chunks=max(1, dff // MLP_CHUNK), final=final),
        grid=(n // tm,),
        in_specs=[
            tile,
            pl.BlockSpec((tm, a.shape[1]), lambda i: (i, 0)),
            _resident(wo.shape),
            _resident((1, d)),
            _resident((d, dff)),
            _resident((dff, d)),
            _resident((1, d)),
        ],
        out_specs=tile,
        out_shape=jax.ShapeDtypeStruct((n, d), F32),
        compiler_params=_params(1),
        name="post_mlp",
    )(h, a, wo, g, wup, wdn, g_final)


def _mix_core_kernel(ug_ref, hp_ref, hc_ref, wg_ref, sc_ref, cw_ref, cb_ref, lg_ref, lb_ref,
                     o_ref, uf_ref, gf_ref, d_ref, *gs_refs, time_axis, nb, tt, pos0, n_t, conv_halo):
    ti = pl.program_id(1)
    pool_w = hp_ref.shape[-1]
    conv_w = hc_ref.shape[-1]
    n_pool_hist = hp_ref.shape[time_axis]
    n_conv_hist = hc_ref.shape[time_axis]
    taps = cw_ref.shape[0]
    group_w = pool_w // len(POOL_WINDOWS)

    def rows(start, size, lanes=slice(None)):
        t = slice(start, start + size)
        return (t, slice(None), lanes) if time_axis == 0 else (slice(None), t, lanes)

    def shape(n_rows, width):
        return (n_rows, nb, width) if time_axis == 0 else (nb, n_rows, width)

    if time_axis == 0:
        rc = max(1, MIX_CHUNK_VREGS * V7X_SUBLANES * V7X_LANES // (nb * conv_w))
    else:
        rc = MIX_CHUNK_VREGS * V7X_SUBLANES * V7X_LANES // conv_w
    rc = min(rc, tt)
    row_iota_shape = (rc, 1, 1) if time_axis == 0 else (1, rc, 1)

    @pl.when(ti == 0)
    def _load_history():
        uf_ref[rows(0, POOL_HALO - n_pool_hist)] = jnp.zeros(shape(POOL_HALO - n_pool_hist, pool_w), F32)
        uf_ref[rows(POOL_HALO - n_pool_hist, n_pool_hist)] = hp_ref[...]
        gf_ref[rows(0, conv_halo - n_conv_hist)] = jnp.zeros(shape(conv_halo - n_conv_hist, conv_w), F32)
        gf_ref[rows(conv_halo - n_conv_hist, n_conv_hist)] = hc_ref[...]

    uf_ref[rows(POOL_HALO, tt)] = ug_ref[:, :, :pool_w]
    gf_ref[rows(conv_halo, tt)] = ug_ref[:, :, pool_w:]

    if time_axis == 1:
        gs_ref, = gs_refs
        n_rows = conv_halo + tt
        for sft in range(1, V7X_SUBLANES):
            gs_ref[sft - 1, :, 0:n_rows - sft, :] = gf_ref[:, sft:n_rows, :]

    def conv_rows(start):
        sft = start % V7X_SUBLANES
        if time_axis == 0 or sft == 0:
            return gf_ref[rows(start, rc)]
        return gs_ref[sft - 1, :, start - sft:start - sft + rc, :]

    cb = cb_ref[...].reshape(1, 1, conv_w)
    lg = lg_ref[...].reshape(1, 1, conv_w)
    lb = lb_ref[...].reshape(1, 1, conv_w)

    for r0 in range(0, tt, rc):
        pos = pos0 + ti * tt + r0 + lax.broadcasted_iota(jnp.int32, row_iota_shape, time_axis)
        for gi, win in enumerate(POOL_WINDOWS):
            lanes = slice(gi * group_w, (gi + 1) * group_w)
            u_new = uf_ref[rows(POOL_HALO + r0, rc, lanes)]
            s = u_new
            for i in range(1, win):
                s = s + uf_ref[rows(POOL_HALO + r0 - i, rc, lanes)]
            cnt = jnp.minimum(win, pos + 1).astype(F32)
            d_ref[rows(r0, rc, lanes)] = s / cnt - u_new

        base = conv_halo - n_conv_hist + r0
        acc = jnp.zeros(shape(rc, conv_w), F32)
        for k in range(taps):
            acc = acc + conv_rows(base + k) * cw_ref[k:k + 1, :].reshape(1, 1, conv_w)
        y = acc + cb
        mu = jnp.mean(y, axis=-1, keepdims=True)
        yc = y - mu
        var = jnp.mean(yc * yc, axis=-1, keepdims=True)
        yn = yc * lax.rsqrt(var + EPS) * lg + lb
        o_ref[rows(r0, rc, slice(pool_w, pool_w + conv_w))] = yn * _sigmoid(yn)

    for gi in range(len(POOL_WINDOWS)):
        lanes = slice(gi * group_w, (gi + 1) * group_w)
        d = d_ref[:, :, lanes].reshape(nb * tt, group_w).astype(BF16)
        y = _dot(d, wg_ref[gi]) * sc_ref[:, lanes]
        o_ref[:, :, lanes] = y.reshape(shape(tt, group_w))

    if n_t > 1:
        uf_ref[rows(0, POOL_HALO)] = uf_ref[rows(tt, POOL_HALO)]
        gf_ref[rows(0, conv_halo)] = gf_ref[rows(tt, conv_halo)]


def _mix_core(ug, hist_pool, hist_conv, w_grp, scale, conv_w, conv_b, ln_g, ln_b, *,
              time_axis, nb, tt, pos0):
    seq_axis = 1 - time_axis
    n_seq, t, width = ug.shape[seq_axis], ug.shape[time_axis], ug.shape[2]
    pool_w = hist_pool.shape[-1]
    cw = hist_conv.shape[-1]
    n_t = t // tt
    conv_halo = -(-hist_conv.shape[time_axis] // V7X_SUBLANES) * V7X_SUBLANES

    def block(n_rows, c):
        if time_axis == 0:
            return pl.BlockSpec((n_rows, nb, c), lambda b, i: (i if n_rows == tt else 0, b, 0))
        return pl.BlockSpec((nb, n_rows, c), lambda b, i: (b, i if n_rows == tt else 0, 0))

    def scratch(n_rows, c):
        return pltpu.VMEM((n_rows, nb, c) if time_axis == 0 else (nb, n_rows, c), F32)

    return pl.pallas_call(
        functools.partial(_mix_core_kernel, time_axis=time_axis, nb=nb, tt=tt, pos0=pos0, n_t=n_t,
                          conv_halo=conv_halo),
        grid=(n_seq // nb, n_t),
        in_specs=[
            block(tt, width),
            block(hist_pool.shape[time_axis], pool_w),
            block(hist_conv.shape[time_axis], cw),
            _resident(w_grp.shape),
            _resident(scale.shape),
            _resident(conv_w.shape),
            _resident(conv_b.shape),
            _resident(ln_g.shape),
            _resident(ln_b.shape),
        ],
        out_specs=block(tt, width),
        out_shape=jax.ShapeDtypeStruct(ug.shape, F32),
        scratch_shapes=[scratch(POOL_HALO + tt, pool_w), scratch(conv_halo + tt, cw), scratch(tt, pool_w)]
        + ([pltpu.VMEM((V7X_SUBLANES - 1, nb, conv_halo + tt, cw), F32)] if time_axis == 1 else []),
        compiler_params=_params(2),
        name="mix_core",
    )(ug, hist_pool, hist_conv, w_grp, scale, conv_w, conv_b, ln_g, ln_b)


def _sb_scores(z, mask):
    sign = jnp.uint32(0x80000000)
    neg_abs = lax.bitcast_convert_type(lax.bitcast_convert_type(z, jnp.uint32) | sign, F32)
    sp = jnp.maximum(z, 0.0) + jnp.log(1.0 + jnp.exp2(neg_abs)) * LOG2E
    log_beta = z - sp
    if mask is not None:
        sp = jnp.where(mask, sp, 0.0)
        log_beta = jnp.where(mask, log_beta, -jnp.inf)
    return sp.astype(BF16), log_beta, sp[:, :1]


def _sb_apply(scores, carry, w, first=0):
    sp, log_beta, sp_first = scores
    later = _dot(sp, w) + carry
    return later[:, first:first + 1] + sp_first, jnp.exp2(log_beta - later)


def _sb_prompt_kernel(bias_ref, q_ref, kt_ref, v_ref, w_ref, o_ref, sp_ref, lb_ref, carry_ref, acc_ref,
                      *, tq, kb, hd):
    qi = pl.program_id(2)
    width = V7X_LANES
    n_cb = q_ref.shape[-1] // width
    heads = width // hd
    lane = lax.broadcasted_iota(jnp.int32, (tq, width), 1)
    tri = lax.broadcasted_iota(jnp.int32, (kb, kb), 1) < lax.broadcasted_iota(jnp.int32, (kb, kb), 0)
    w = w_ref[...]
    in_head = [(lane >= hh * hd) & (lane < (hh + 1) * hd) for hh in range(heads)]
    one_lanes = jnp.where(lane < 2, 1.0, 0.0).astype(BF16)
    krow = lax.broadcasted_iota(jnp.int32, (width, kb), 0)
    groups = []
    for cb in range(n_cb):
        q = q_ref[:, cb * width:(cb + 1) * width].astype(F32)
        for hh in range(heads):
            qa = jnp.concatenate([jnp.where(in_head[hh], q, 0.0).astype(BF16), one_lanes], axis=1)
            b = jnp.full((width, kb), bias_ref[(pl.program_id(1) * n_cb + cb) * heads + hh], F32)
            b_hi = b.astype(BF16).astype(F32)
            rows = jnp.where(krow == 0, b_hi, jnp.where(krow == 1, b - b_hi, 0.0)).astype(BF16)
            groups.append((cb, qa, rows))

    def score(slot, blk, r0, r1, mask):
        start = pl.multiple_of(blk * kb, kb)
        for g, (cb, qa, bias_rows) in enumerate(groups):
            kt = kt_ref[cb * width:(cb + 1) * width, pl.ds(start, kb)]
            z = _dot(qa[r0:r1], jnp.concatenate([kt, bias_rows], axis=0))
            sp, log_beta, _ = _sb_scores(z, mask)
            sp_ref[slot, g, r0:r1] = sp
            lb_ref[slot, g, r0:r1] = log_beta

    def accumulate(slot, blk, r0, r1):
        start = pl.multiple_of(blk * kb, kb)
        for g, (cb, _, _) in enumerate(groups):
            sp = sp_ref[slot, g, r0:r1]
            scores = (sp, lb_ref[slot, g, r0:r1], sp[:, :1].astype(F32))
            carry_ref[g, r0:r1], a = _sb_apply(scores, carry_ref[g, r0:r1], w)
            acc_ref[g, r0:r1] += _dot(a.astype(BF16), v_ref[pl.ds(start, kb), cb * width:(cb + 1) * width])

    first = 2 * qi
    carry_ref[...] = jnp.zeros(carry_ref.shape, F32)
    acc_ref[...] = jnp.zeros(acc_ref.shape, F32)
    score(0, first + 1, kb, tq, tri)
    score(1, first, 0, kb, tri)
    score(1, first, kb, tq, None)
    accumulate(0, first + 1, kb, tq)

    @pl.loop(0, qi)
    def _(i):
        blk = first - 1 - 2 * i
        score(0, blk, 0, tq, None)
        accumulate(1, blk + 1, 0, tq)
        score(1, blk - 1, 0, tq, None)
        accumulate(0, blk, 0, tq)

    accumulate(1, 0, 0, tq)
    for cb in range(n_cb):
        out = acc_ref[cb * heads]
        for hh in range(1, heads):
            out = jnp.where(in_head[hh], acc_ref[cb * heads + hh], out)
        o_ref[:, cb * width:(cb + 1) * width] = out


def _sb_prompt(q, kt, v, bias, w, *, hd):
    b, s, d = q.shape
    kb = w.shape[1]
    tq = 2 * kb
    cols = ATTN_COL_BLOCKS * V7X_LANES
    n_groups = cols // hd
    qtile = pl.BlockSpec((None, tq, cols), lambda bi, p, i, *_: (bi, i, p))
    return pl.pallas_call(
        functools.partial(_sb_prompt_kernel, tq=tq, kb=kb, hd=hd),
        grid_spec=pltpu.PrefetchScalarGridSpec(
            num_scalar_prefetch=1,
            grid=(b, d // cols, s // tq),
            in_specs=[
                qtile,
                pl.BlockSpec((None, cols, s), lambda bi, p, i, *_: (bi, p, 0)),
                pl.BlockSpec((None, s, cols), lambda bi, p, i, *_: (bi, 0, p)),
                _resident(w.shape),
            ],
            out_specs=qtile,
            scratch_shapes=[
                pltpu.VMEM((2, n_groups, tq, kb), BF16),
                pltpu.VMEM((2, n_groups, tq, kb), F32),
                pltpu.VMEM((n_groups, tq, 1), F32),
                pltpu.VMEM((n_groups, tq, V7X_LANES), F32),
            ],
        ),
        out_shape=jax.ShapeDtypeStruct((b, s, d), F32),
        compiler_params=_params(3),
        name="sb_prompt",
    )(bias, q, kt, v, w)


def _sb_sample_kernel(pt_ref, q_ref, kn_ref, vn_ref, *refs, n_heads, hd, t_new, n_slots):
    kc_refs, vc_refs = refs[:n_slots], refs[n_slots:2 * n_slots]
    w_ref, wall_ref, b_ref, o_ref, qbd_ref, carry_ref, new_ref, acc_ref = refs[2 * n_slots:]
    j = pl.program_id(1)
    rows = n_heads * t_new
    page = kc_refs[0].shape[-1]
    d = q_ref.shape[-1]
    w = w_ref[...]
    bias = b_ref[...]

    @pl.when(j == 0)
    def _new_keys():
        q = jnp.broadcast_to(q_ref[...][None], (n_heads, t_new, d)).reshape(rows, d)
        r = lax.broadcasted_iota(jnp.int32, (rows, d), 0)
        c = lax.broadcasted_iota(jnp.int32, (rows, d), 1)
        qbd = jnp.where(r // t_new == c // hd, q, 0.0).astype(BF16)
        qbd_ref[...] = qbd
        pad = jnp.zeros((page - t_new, d), F32)
        kn = jnp.concatenate([kn_ref[...], pad], axis=0).astype(BF16)
        vn = jnp.concatenate([vn_ref[...], pad], axis=0).astype(BF16)
        rr = lax.broadcasted_iota(jnp.int32, (rows, page), 0)
        cc = lax.broadcasted_iota(jnp.int32, (rows, page), 1)
        z = _dot_nt(qbd, kn) + bias
        carry, a = _sb_apply(_sb_scores(z, cc < rr % t_new), jnp.zeros((rows, 1), F32), w)
        carry_ref[...] = carry
        new_ref[...] = _dot(a.astype(BF16), vn)
        acc_ref[...] = jnp.zeros(acc_ref.shape, F32)

    kt = jnp.concatenate([kc_ref[...].astype(BF16) for kc_ref in kc_refs], axis=1)
    vt = jnp.concatenate([vc_ref[...].astype(BF16) for vc_ref in vc_refs], axis=1)
    z = _dot(qbd_ref[...], kt) + jnp.concatenate([bias] * n_slots, axis=1)
    sp, log_beta, _ = _sb_scores(z, None)
    oldest = (n_slots - 1) * page
    scores = (sp, log_beta, sp[:, oldest:oldest + 1].astype(F32))
    carry, a = _sb_apply(scores, carry_ref[...], wall_ref[...], first=oldest)
    carry_ref[...] = carry
    acc_ref[...] += _dot(vt, a.T.astype(BF16))

    @pl.when(j == pl.num_programs(1) - 1)
    def _emit():
        total = new_ref[...] + acc_ref[...].T
        head_of_col = lax.broadcasted_iota(jnp.int32, (t_new, d), 1) // hd
        out = jnp.zeros((t_new, d), F32)
        for h in range(n_heads):
            out = jnp.where(head_of_col == h, total[h * t_new:(h + 1) * t_new, :], out)
        o_ref[...] = out


def _sb_sample(page_table, q, k_new, v_new, cache_kt, cache_vt, layer, bias_rows, w, *, n_heads, hd):
    db, t_new, d = q.shape
    n_pages = page_table.shape[1]
    page = cache_kt.shape[-1]
    rows = n_heads * t_new
    n_slots = min(SAMPLE_PAGES_PER_STEP, n_pages)
    seq = pl.BlockSpec((None, t_new, d), lambda b, j, pt: (b, 0, 0))

    def cache(slot):
        return pl.BlockSpec(
            (None, None, d, page),
            lambda b, j, pt: (layer, pt[b, n_pages - 1 - (j * n_slots + slot)], 0, 0))

    slots = [cache(u) for u in range(n_slots)]
    kj = lax.broadcasted_iota(jnp.int32, (n_slots * page, n_slots * page), 0)
    ks = lax.broadcasted_iota(jnp.int32, (n_slots * page, n_slots * page), 1)
    w_step = ((kj // page < ks // page) | ((kj // page == ks // page) & (kj > ks))).astype(BF16)
    return pl.pallas_call(
        functools.partial(_sb_sample_kernel, n_heads=n_heads, hd=hd, t_new=t_new, n_slots=n_slots),
        grid_spec=pltpu.PrefetchScalarGridSpec(
            num_scalar_prefetch=1,
            grid=(db, n_pages // n_slots),
            in_specs=[seq, seq, seq] + slots + slots + [
                _resident(w.shape), _resident(w_step.shape), _resident(bias_rows.shape),
            ],
            out_specs=seq,
            scratch_shapes=[
                pltpu.VMEM((rows, d), BF16),
                pltpu.VMEM((rows, 1), F32),
                pltpu.VMEM((rows, d), F32),
                pltpu.VMEM((d, rows), F32),
            ],
        ),
        out_shape=jax.ShapeDtypeStruct((db, t_new, d), F32),
        compiler_params=_params(2),
        name="sb_sample",
    )(page_table, q, k_new, v_new, *[cache_kt] * n_slots, *[cache_vt] * n_slots, w, w_step, bias_rows)


def _suffix_sum_matrix(kb):
    j = lax.broadcasted_iota(jnp.int32, (kb, kb), 0)
    s = lax.broadcasted_iota(jnp.int32, (kb, kb), 1)
    return (j > s).astype(BF16)


def kernel(x_prompt, x_sample, cache_k, cache_v, state_pool, state_conv, page_table, norm_mix, norm_mlp, norm_final, w_in_mix, w_out_mix, w_pool_grp, pool_scale, conv_w, conv_b, conv_ln_g, conv_ln_b, w_qkv, w_o, sb_bias, w_up, w_down):
    b, s, d = x_prompt.shape
    db, t_new, _ = x_sample.shape
    depth = norm_mix.shape[0]
    n_sb, n_pool_pages, page, n_heads, hd = cache_k.shape
    past_len = page_table.shape[1] * page
    pool_w = state_pool.shape[-1]
    conv_width = state_conv.shape[-1]
    n_pool_hist = state_pool.shape[2]
    n_conv_hist = state_conv.shape[2]
    q_scale = float(hd) ** -0.5 * LOG2E
    bias2 = sb_bias * LOG2E

    row = lambda x: x.reshape(1, -1)
    hp = x_prompt.reshape(b * s, d)
    hs = x_sample.transpose(1, 0, 2).reshape(t_new * db, d)
    cache_kt = cache_k.transpose(0, 1, 3, 4, 2).reshape(n_sb, n_pool_pages, d, page)
    cache_vt = cache_v.transpose(0, 1, 3, 4, 2).reshape(n_sb, n_pool_pages, d, page)
    state_pool_t = state_pool.transpose(0, 2, 1, 3)
    state_conv_t = state_conv.transpose(0, 2, 1, 3)
    w_sums_prompt = _suffix_sum_matrix(min(ATTN_K_BLOCK, s // 2))
    w_sums_sample = _suffix_sum_matrix(page)
    g_final = row(norm_final)

    kt_p = vt_p = kt_s = vt_s = None
    pool_p, pool_s, conv_p, conv_s = [], [], [], []
    for l in range(depth):
        final = l == depth - 1
        g_mix = row(norm_mix[l])
        mlp_w = (row(norm_mlp[l]), w_up[l].astype(BF16), w_down[l].astype(BF16), g_final, final)
        if l % 2 == 0:
            m = l // 2
            w_in = w_in_mix[m].astype(BF16)
            w_out = w_out_mix[m].astype(BF16)
            core_w = (w_pool_grp[m].astype(BF16), row(pool_scale[m]), conv_w[m], row(conv_b[m]),
                      row(conv_ln_g[m]), row(conv_ln_b[m]))

            ug_p = _mix_in(hp, g_mix, w_in).reshape(b, s, pool_w + conv_width)
            ug_s = _mix_in(hs, g_mix, w_in).reshape(t_new, db, pool_w + conv_width)
            cat_p = _mix_core(ug_p, jnp.zeros((b, n_pool_hist, pool_w), F32),
                              jnp.zeros((b, n_conv_hist, conv_width), F32), *core_w,
                              time_axis=1, nb=1, tt=min(TOKEN_TILE, s), pos0=0)
            cat_s = _mix_core(ug_s, state_pool_t[m], state_conv_t[m], *core_w,
                              time_axis=0, nb=min(SAMPLE_SEQS, db), tt=t_new, pos0=past_len)
            pool_p.append(ug_p[:, s - n_pool_hist:, :pool_w])
            conv_p.append(ug_p[:, s - n_conv_hist:, pool_w:])
            pool_s.append(jnp.concatenate([state_pool_t[m], ug_s[:, :, :pool_w]], axis=0)[-n_pool_hist:])
            conv_s.append(jnp.concatenate([state_conv_t[m], ug_s[:, :, pool_w:]], axis=0)[-n_conv_hist:])
            hp = _post(hp, cat_p.reshape(b * s, -1), w_out, *mlp_w)
            hs = _post(hs, cat_s.reshape(t_new * db, -1), w_out, *mlp_w)
        else:
            a = l // 2
            wq, wk, wv = (w_qkv[a][:, i * d:(i + 1) * d].astype(BF16) for i in range(3))
            qkv_w = (wq, wk, wv, wk.T, wv.T, q_scale)
            wo = w_o[a].astype(BF16)

            q, _, vb, ktb, kt_p, vt_p = _qkv(hp, g_mix, *qkv_w, kt_p, vt_p, a, n_sb, b)
            o_p = _sb_prompt(q.reshape(b, s, d), ktb, vb.reshape(b, s, d), bias2[a], w_sums_prompt, hd=hd)
            hp = _post(hp, o_p.reshape(b * s, d), wo, *mlp_w)

            q, kb, vb, _, kt_s, vt_s = _qkv(hs, g_mix, *qkv_w, kt_s, vt_s, a, n_sb, t_new)
            by_seq = lambda x: x.reshape(t_new, db, d).transpose(1, 0, 2).astype(F32)
            bias_rows = jnp.broadcast_to(jnp.repeat(bias2[a], t_new)[:, None], (n_heads * t_new, page))
            o_s = _sb_sample(page_table, by_seq(q), by_seq(kb), by_seq(vb), cache_kt, cache_vt, a,
                             bias_rows, w_sums_sample, n_heads=n_heads, hd=hd)
            hs = _post(hs, o_s.transpose(1, 0, 2).reshape(t_new * db, d), wo, *mlp_w)

    kv_prompt = lambda x: x.reshape(n_sb, b, n_heads, hd, s).transpose(0, 1, 4, 2, 3)
    kv_sample = lambda x: x.reshape(n_sb, t_new, n_heads, hd, db).transpose(0, 4, 1, 2, 3)
    state_sample = lambda xs: jnp.stack(xs).transpose(0, 2, 1, 3)
    return (hp.reshape(b, s, d), hs.reshape(t_new, db, d).transpose(1, 0, 2),
            kv_prompt(kt_p), kv_prompt(vt_p), kv_sample(kt_s), kv_sample(vt_s),
            jnp.stack(pool_p), state_sample(pool_s), jnp.stack(conv_p), state_sample(conv_s))
```

```python
import functools

import jax
import jax.numpy as jnp
from jax import lax
from jax.experimental import pallas as pl
from jax.experimental.pallas import tpu as pltpu

F32 = jnp.float32
BF16 = jnp.bfloat16

EPS = 1e-6
POOL_WINDOWS = (2, 4, 8, 16)

V7X_SUBLANES = 8
V7X_LANES = 128
V7X_VMEM_BYTES = 64 * 1024 * 1024
VMEM_LIMIT_BYTES = V7X_VMEM_BYTES * 7 // 8

TOKEN_TILE = 512
MLP_CHUNK = 1024
MIX_CHUNK_VREGS = 16
ATTN_K_BLOCK = 256
ATTN_COL_BLOCKS = 4
SAMPLE_PAGES_PER_STEP = 8
LOG2E = 1.4426950408889634
POOL_HALO = 16
SAMPLE_SEQS = 32


def _params(n_axes):
    return pltpu.CompilerParams(
        dimension_semantics=("arbitrary",) * n_axes,
        vmem_limit_bytes=VMEM_LIMIT_BYTES,
    )


def _resident(shape):
    zeros = (0,) * len(shape)
    return pl.BlockSpec(shape, lambda *_: zeros, pipeline_mode=pl.Buffered(1))


def _layer_resident(stack, layer):
    tail = (0,) * (stack.ndim - 1)
    return pl.BlockSpec((None,) + stack.shape[1:], lambda *_: (layer,) + tail, pipeline_mode=pl.Buffered(1))


def _rmsnorm(x, g):
    return x * lax.rsqrt(jnp.mean(x * x, axis=-1, keepdims=True) + EPS) * g


def _sigmoid(x):
    return 1.0 / (1.0 + jnp.exp(-x))


def _dot(a, b):
    return jnp.dot(a, b, preferred_element_type=F32)


def _dot_nt(a, b):
    return lax.dot_general(a, b, (((1,), (1,)), ((), ())), preferred_element_type=F32)


def _mix_in_kernel(x_ref, g_ref, w_ref, o_ref, *, pool_w, conv_w):
    xn = _rmsnorm(x_ref[...], g_ref[...]).astype(BF16)
    o_ref[:, :pool_w] = _dot(xn, w_ref[:, :pool_w])
    a = _dot(xn, w_ref[:, pool_w:pool_w + conv_w])
    gate = _dot(xn, w_ref[:, pool_w + conv_w:])
    o_ref[:, pool_w:] = a * _sigmoid(gate)


def _mix_in(h, g, w, layer):
    n, d = h.shape
    cols = w.shape[2]
    pool_w = conv_w = cols // 3
    tm = min(TOKEN_TILE, n)
    return pl.pallas_call(
        functools.partial(_mix_in_kernel, pool_w=pool_w, conv_w=conv_w),
        grid=(n // tm,),
        in_specs=[
            pl.BlockSpec((tm, d), lambda i: (i, 0)),
            _resident((1, d)),
            _layer_resident(w, layer),
        ],
        out_specs=pl.BlockSpec((tm, pool_w + conv_w), lambda i: (i, 0)),
        out_shape=jax.ShapeDtypeStruct((n, pool_w + conv_w), F32),
        compiler_params=_params(1),
        name="mix_in",
    )(h, g, w)


def _qkv_kernel(x_ref, g_ref, wq_ref, wk_ref, wv_ref, wkt_ref, wvt_ref, *refs, q_scale, has_alias):
    q_ref, kb_ref, vb_ref, ktb_ref, kt_ref, vt_ref = refs[2 if has_alias else 0:]
    xn = _rmsnorm(x_ref[...], g_ref[...]).astype(BF16)
    q_ref[...] = (_dot(xn, wq_ref[...]) * q_scale).astype(BF16)
    kb_ref[...] = _dot(xn, wk_ref[...]).astype(BF16)
    vb_ref[...] = _dot(xn, wv_ref[...]).astype(BF16)
    kt = _dot_nt(wkt_ref[...], xn)
    kt_ref[...] = kt
    ktb_ref[...] = kt.astype(BF16)
    vt_ref[...] = _dot_nt(wvt_ref[...], xn)


def _qkv(h, g, wq, wk, wv, wkt, wvt, q_scale, kt_all, vt_all, layer, n_layers, groups):
    n, d = h.shape
    glen = n // groups
    tm = min(TOKEN_TILE, glen)
    per_group = glen // tm
    tile = pl.BlockSpec((tm, d), lambda i: (i, 0))
    t_tile = pl.BlockSpec((None, None, d, tm), lambda i: (layer, i // per_group, 0, i % per_group))
    tb_tile = pl.BlockSpec((None, d, tm), lambda i: (i // per_group, 0, i % per_group))
    weights = [_resident((d, d))] * 5
    has_alias = kt_all is not None
    alias_args = (kt_all, vt_all) if has_alias else ()
    alias_specs = [pl.BlockSpec(memory_space=pl.ANY)] * len(alias_args)
    t_shape = jax.ShapeDtypeStruct((n_layers, groups, d, glen), F32)
    return pl.pallas_call(
        functools.partial(_qkv_kernel, q_scale=q_scale, has_alias=has_alias),
        grid=(n // tm,),
        in_specs=[tile, _resident((1, d))] + weights + alias_specs,
        out_specs=[tile, tile, tile, tb_tile, t_tile, t_tile],
        out_shape=[jax.ShapeDtypeStruct((n, d), BF16)] * 3
        + [jax.ShapeDtypeStruct((groups, d, glen), BF16), t_shape, t_shape],
        input_output_aliases={7: 4, 8: 5} if has_alias else {},
        compiler_params=_params(1),
        name="qkv",
    )(h, g, wq, wk, wv, wkt, wvt, *alias_args)


def _post_kernel(h_ref, a_ref, wo_ref, g_ref, wup_ref, wdn_ref, gf_ref, o_ref, *, n_chunks, final):
    h1 = h_ref[...] + _dot(a_ref[...].astype(BF16), wo_ref[...])
    xn = _rmsnorm(h1, g_ref[...]).astype(BF16)
    ck = wup_ref.shape[1] // n_chunks
    acc = h1
    for c in range(n_chunks):
        up = _dot(xn, wup_ref[:, c * ck:(c + 1) * ck])
        act = jnp.square(jnp.maximum(up, 0.0)).astype(BF16)
        acc = acc + _dot(act, wdn_ref[c * ck:(c + 1) * ck, :])
    if final:
        acc = _rmsnorm(acc, gf_ref[...])
    o_ref[...] = acc


def _post(h, a, wo, wo_layer, g, wup, wdn, mlp_layer, g_final, final):
    n, d = h.shape
    dff = wup.shape[2]
    tm = min(TOKEN_TILE, n)
    tile = pl.BlockSpec((tm, d), lambda i: (i, 0))
    return pl.pallas_call(
        functools.partial(_post_kernel, n_chunks=max(1, dff // MLP_CHUNK), final=final),
        grid=(n // tm,),
        in_specs=[
            tile,
            pl.BlockSpec((tm, a.shape[1]), lambda i: (i, 0)),
            _layer_resident(wo, wo_layer),
            _resident((1, d)),
            _layer_resident(wup, mlp_layer),
            _layer_resident(wdn, mlp_layer),
            _resident((1, d)),
        ],
        out_specs=tile,
        out_shape=jax.ShapeDtypeStruct((n, d), F32),
        compiler_params=_params(1),
        name="post_mlp",
    )(h, a, wo, g, wup, wdn, g_final)


def _mix_core_kernel(ug_ref, hp_ref, hc_ref, wg_ref, sc_ref, cw_ref, cb_ref, lg_ref, lb_ref,
                     o_ref, uf_ref, gf_ref, d_ref, *gs_refs, time_axis, nb, tt, pos0, n_t, conv_halo):
    ti = pl.program_id(1)
    pool_w = hp_ref.shape[-1]
    conv_w = hc_ref.shape[-1]
    n_pool_hist = hp_ref.shape[time_axis]
    n_conv_hist = hc_ref.shape[time_axis]
    taps = cw_ref.shape[0]
    group_w = pool_w // len(POOL_WINDOWS)

    def rows(start, size, lanes=slice(None)):
        t = slice(start, start + size)
        return (t, slice(None), lanes) if time_axis == 0 else (slice(None), t, lanes)

    def shape(n_rows, width):
        return (n_rows, nb, width) if time_axis == 0 else (nb, n_rows, width)

    if time_axis == 0:
        rc = max(1, MIX_CHUNK_VREGS * V7X_SUBLANES * V7X_LANES // (nb * conv_w))
    else:
        rc = MIX_CHUNK_VREGS * V7X_SUBLANES * V7X_LANES // conv_w
    rc = min(rc, tt)
    row_iota_shape = (rc, 1, 1) if time_axis == 0 else (1, rc, 1)

    @pl.when(ti == 0)
    def _load_history():
        uf_ref[rows(0, POOL_HALO - n_pool_hist)] = jnp.zeros(shape(POOL_HALO - n_pool_hist, pool_w), F32)
        uf_ref[rows(POOL_HALO - n_pool_hist, n_pool_hist)] = hp_ref[...]
        gf_ref[rows(0, conv_halo - n_conv_hist)] = jnp.zeros(shape(conv_halo - n_conv_hist, conv_w), F32)
        gf_ref[rows(conv_halo - n_conv_hist, n_conv_hist)] = hc_ref[...]

    uf_ref[rows(POOL_HALO, tt)] = ug_ref[:, :, :pool_w]
    gf_ref[rows(conv_halo, tt)] = ug_ref[:, :, pool_w:]

    if time_axis == 1:
        gs_ref, = gs_refs
        n_rows = conv_halo + tt
        for sft in range(1, V7X_SUBLANES):
            gs_ref[sft - 1, :, 0:n_rows - sft, :] = gf_ref[:, sft:n_rows, :]

    def conv_rows(start):
        sft = start % V7X_SUBLANES
        if time_axis == 0 or sft == 0:
            return gf_ref[rows(start, rc)]
        return gs_ref[sft - 1, :, start - sft:start - sft + rc, :]

    cb = cb_ref[...].reshape(1, 1, conv_w)
    lg = lg_ref[...].reshape(1, 1, conv_w)
    lb = lb_ref[...].reshape(1, 1, conv_w)

    for r0 in range(0, tt, rc):
        pos = pos0 + ti * tt + r0 + lax.broadcasted_iota(jnp.int32, row_iota_shape, time_axis)
        for gi, win in enumerate(POOL_WINDOWS):
            lanes = slice(gi * group_w, (gi + 1) * group_w)
            u_new = uf_ref[rows(POOL_HALO + r0, rc, lanes)]
            s = u_new
            for i in range(1, win):
                s = s + uf_ref[rows(POOL_HALO + r0 - i, rc, lanes)]
            cnt = jnp.minimum(win, pos + 1).astype(F32)
            d_ref[rows(r0, rc, lanes)] = s / cnt - u_new

        base = conv_halo - n_conv_hist + r0
        acc = jnp.zeros(shape(rc, conv_w), F32)
        for k in range(taps):
            acc = acc + conv_rows(base + k) * cw_ref[k:k + 1, :].reshape(1, 1, conv_w)
        y = acc + cb
        mu = jnp.mean(y, axis=-1, keepdims=True)
        yc = y - mu
        var = jnp.mean(yc * yc, axis=-1, keepdims=True)
        yn = yc * lax.rsqrt(var + EPS) * lg + lb
        o_ref[rows(r0, rc, slice(pool_w, pool_w + conv_w))] = yn * _sigmoid(yn)

    for gi in range(len(POOL_WINDOWS)):
        lanes = slice(gi * group_w, (gi + 1) * group_w)
        d = d_ref[:, :, lanes].reshape(nb * tt, group_w).astype(BF16)
        y = _dot(d, wg_ref[gi]) * sc_ref[:, lanes]
        o_ref[:, :, lanes] = y.reshape(shape(tt, group_w))

    if n_t > 1:
        uf_ref[rows(0, POOL_HALO)] = uf_ref[rows(tt, POOL_HALO)]
        gf_ref[rows(0, conv_halo)] = gf_ref[rows(tt, conv_halo)]


def _mix_core(ug, hist_pool, hist_conv, w_grp, scale, conv_w, conv_b, ln_g, ln_b, *,
              time_axis, nb, tt, pos0):
    seq_axis = 1 - time_axis
    n_seq, t, width = ug.shape[seq_axis], ug.shape[time_axis], ug.shape[2]
    pool_w = hist_pool.shape[-1]
    cw = hist_conv.shape[-1]
    n_t = t // tt
    conv_halo = -(-hist_conv.shape[time_axis] // V7X_SUBLANES) * V7X_SUBLANES

    def block(n_rows, c):
        if time_axis == 0:
            return pl.BlockSpec((n_rows, nb, c), lambda b, i: (i if n_rows == tt else 0, b, 0))
        return pl.BlockSpec((nb, n_rows, c), lambda b, i: (b, i if n_rows == tt else 0, 0))

    def scratch(n_rows, c):
        return pltpu.VMEM((n_rows, nb, c) if time_axis == 0 else (nb, n_rows, c), F32)

    return pl.pallas_call(
        functools.partial(_mix_core_kernel, time_axis=time_axis, nb=nb, tt=tt, pos0=pos0, n_t=n_t,
                          conv_halo=conv_halo),
        grid=(n_seq // nb, n_t),
        in_specs=[
            block(tt, width),
            block(hist_pool.shape[time_axis], pool_w),
            block(hist_conv.shape[time_axis], cw),
            _resident(w_grp.shape),
            _resident(scale.shape),
            _resident(conv_w.shape),
            _resident(conv_b.shape),
            _resident(ln_g.shape),
            _resident(ln_b.shape),
        ],
        out_specs=block(tt, width),
        out_shape=jax.ShapeDtypeStruct(ug.shape, F32),
        scratch_shapes=[scratch(POOL_HALO + tt, pool_w), scratch(conv_halo + tt, cw), scratch(tt, pool_w)]
        + ([pltpu.VMEM((V7X_SUBLANES - 1, nb, conv_halo + tt, cw), F32)] if time_axis == 1 else []),
        compiler_params=_params(2),
        name="mix_core",
    )(ug, hist_pool, hist_conv, w_grp, scale, conv_w, conv_b, ln_g, ln_b)


def _sb_scores(z, mask):
    sign = jnp.uint32(0x80000000)
    neg_abs = lax.bitcast_convert_type(lax.bitcast_convert_type(z, jnp.uint32) | sign, F32)
    sp = jnp.maximum(z, 0.0) + jnp.log(1.0 + jnp.exp2(neg_abs)) * LOG2E
    log_beta = z - sp
    if mask is not None:
        sp = jnp.where(mask, sp, 0.0)
        log_beta = jnp.where(mask, log_beta, -jnp.inf)
    return sp.astype(BF16), log_beta, sp[:, :1]


def _sb_apply(scores, carry, w, first=0):
    sp, log_beta, sp_first = scores
    later = _dot(sp, w) + carry
    return later[:, first:first + 1] + sp_first, jnp.exp2(log_beta - later)


def _sb_prompt_kernel(bias_ref, q_ref, kt_ref, v_ref, w_ref, o_ref, sp_ref, lb_ref, carry_ref, acc_ref,
                      *, tq, kb, hd):
    qi = pl.program_id(2)
    width = V7X_LANES
    n_cb = q_ref.shape[-1] // width
    heads = width // hd
    lane = lax.broadcasted_iota(jnp.int32, (tq, width), 1)
    tri = lax.broadcasted_iota(jnp.int32, (kb, kb), 1) < lax.broadcasted_iota(jnp.int32, (kb, kb), 0)
    w = w_ref[...]
    in_head = [(lane >= hh * hd) & (lane < (hh + 1) * hd) for hh in range(heads)]
    one_lanes = jnp.where(lane < 2, 1.0, 0.0).astype(BF16)
    krow = lax.broadcasted_iota(jnp.int32, (width, kb), 0)
    groups = []
    for cb in range(n_cb):
        q = q_ref[:, cb * width:(cb + 1) * width].astype(F32)
        for hh in range(heads):
            qa = jnp.concatenate([jnp.where(in_head[hh], q, 0.0).astype(BF16), one_lanes], axis=1)
            b = jnp.full((width, kb), bias_ref[(pl.program_id(1) * n_cb + cb) * heads + hh], F32)
            b_hi = b.astype(BF16).astype(F32)
            rows = jnp.where(krow == 0, b_hi, jnp.where(krow == 1, b - b_hi, 0.0)).astype(BF16)
            groups.append((cb, qa, rows))

    def score(slot, blk, r0, r1, mask):
        start = pl.multiple_of(blk * kb, kb)
        for g, (cb, qa, bias_rows) in enumerate(groups):
            kt = kt_ref[cb * width:(cb + 1) * width, pl.ds(start, kb)]
            z = _dot(qa[r0:r1], jnp.concatenate([kt, bias_rows], axis=0))
            sp, log_beta, _ = _sb_scores(z, mask)
            sp_ref[slot, g, r0:r1] = sp
            lb_ref[slot, g, r0:r1] = log_beta

    def accumulate(slot, blk, r0, r1):
        start = pl.multiple_of(blk * kb, kb)
        for g, (cb, _, _) in enumerate(groups):
            sp = sp_ref[slot, g, r0:r1]
            scores = (sp, lb_ref[slot, g, r0:r1], sp[:, :1].astype(F32))
            carry_ref[g, r0:r1], a = _sb_apply(scores, carry_ref[g, r0:r1], w)
            acc_ref[g, r0:r1] += _dot(a.astype(BF16), v_ref[pl.ds(start, kb), cb * width:(cb + 1) * width])

    first = 2 * qi
    carry_ref[...] = jnp.zeros(carry_ref.shape, F32)
    acc_ref[...] = jnp.zeros(acc_ref.shape, F32)
    score(0, first + 1, kb, tq, tri)
    score(1, first, 0, kb, tri)
    score(1, first, kb, tq, None)
    accumulate(0, first + 1, kb, tq)

    @pl.loop(0, qi)
    def _(i):
        blk = first - 1 - 2 * i
        score(0, blk, 0, tq, None)
        accumulate(1, blk + 1, 0, tq)
        score(1, blk - 1, 0, tq, None)
        accumulate(0, blk, 0, tq)

    accumulate(1, 0, 0, tq)
    for cb in range(n_cb):
        out = acc_ref[cb * heads]
        for hh in range(1, heads):
            out = jnp.where(in_head[hh], acc_ref[cb * heads + hh], out)
        o_ref[:, cb * width:(cb + 1) * width] = out


def _sb_prompt(q, kt, v, bias, w, *, hd):
    b, s, d = q.shape
    kb = w.shape[1]
    tq = 2 * kb
    cols = ATTN_COL_BLOCKS * V7X_LANES
    n_groups = cols // hd
    qtile = pl.BlockSpec((None, tq, cols), lambda bi, p, i, *_: (bi, i, p))
    return pl.pallas_call(
        functools.partial(_sb_prompt_kernel, tq=tq, kb=kb, hd=hd),
        grid_spec=pltpu.PrefetchScalarGridSpec(
            num_scalar_prefetch=1,
            grid=(b, d // cols, s // tq),
            in_specs=[
                qtile,
                pl.BlockSpec((None, cols, s), lambda bi, p, i, *_: (bi, p, 0)),
                pl.BlockSpec((None, s, cols), lambda bi, p, i, *_: (bi, 0, p)),
                _resident(w.shape),
            ],
            out_specs=qtile,
            scratch_shapes=[
                pltpu.VMEM((2, n_groups, tq, kb), BF16),
                pltpu.VMEM((2, n_groups, tq, kb), F32),
                pltpu.VMEM((n_groups, tq, 1), F32),
                pltpu.VMEM((n_groups, tq, V7X_LANES), F32),
            ],
        ),
        out_shape=jax.ShapeDtypeStruct((b, s, d), F32),
        compiler_params=_params(3),
        name="sb_prompt",
    )(bias, q, kt, v, w)


def _sb_sample_kernel(pt_ref, q_ref, kn_ref, vn_ref, *refs, n_heads, hd, t_new, n_slots):
    kc_refs, vc_refs = refs[:n_slots], refs[n_slots:2 * n_slots]
    w_ref, wall_ref, b_ref, o_ref, qcb_ref, carry_ref, acc_ref = refs[2 * n_slots:]
    j = pl.program_id(1)
    rows = n_heads * t_new
    page = kc_refs[0].shape[-1]
    d = q_ref.shape[-1]
    width = V7X_LANES
    n_cb = d // width
    cb_heads = width // hd
    cb_rows = cb_heads * t_new
    bias = b_ref[...]
    lane = lax.broadcasted_iota(jnp.int32, (t_new, width), 1)

    def cols(cb):
        return slice(cb * width, (cb + 1) * width)

    def head_rows(cb):
        return slice(cb * cb_rows, (cb + 1) * cb_rows)

    @pl.when(j == 0)
    def _new_keys():
        for cb in range(n_cb):
            q = q_ref[:, cols(cb)]
            qcb_ref[cb] = jnp.concatenate(
                [jnp.where(lane // hd == hh, q, 0.0) for hh in range(cb_heads)], axis=0).astype(BF16)
        pad = jnp.zeros((page - t_new, d), F32)
        kn = jnp.concatenate([kn_ref[...], pad], axis=0).astype(BF16)
        vn = jnp.concatenate([vn_ref[...], pad], axis=0).astype(BF16)
        rr = lax.broadcasted_iota(jnp.int32, (rows, page), 0)
        cc = lax.broadcasted_iota(jnp.int32, (rows, page), 1)
        z = jnp.concatenate([_dot_nt(qcb_ref[cb], kn[:, cols(cb)]) for cb in range(n_cb)], axis=0) + bias
        carry, a = _sb_apply(_sb_scores(z, cc < rr % t_new), jnp.zeros((rows, 1), F32), w_ref[...])
        carry_ref[...] = carry
        a = a.astype(BF16)
        for cb in range(n_cb):
            acc_ref[cb] = _dot(a[head_rows(cb)], vn[:, cols(cb)])

    kt = jnp.concatenate([kc_ref[...].astype(BF16) for kc_ref in kc_refs], axis=1)
    vt = jnp.concatenate([vc_ref[...].astype(BF16) for vc_ref in vc_refs], axis=1)
    z = jnp.concatenate([_dot(qcb_ref[cb], kt[cols(cb)]) for cb in range(n_cb)], axis=0)
    sp, log_beta, _ = _sb_scores(z + jnp.concatenate([bias] * n_slots, axis=1), None)
    tile = 2 * page
    later = jnp.concatenate(
        [_dot(sp[:, :(c + 1) * tile], wall_ref[:(c + 1) * tile, c * tile:(c + 1) * tile])
         for c in range(n_slots * page // tile)], axis=1) + carry_ref[...]
    a = jnp.exp2(log_beta - later).astype(BF16)
    oldest = (n_slots - 1) * page
    carry_ref[...] = later[:, oldest:oldest + 1] + sp[:, oldest:oldest + 1].astype(F32)
    for cb in range(n_cb):
        acc_ref[cb] += _dot_nt(a[head_rows(cb)], vt[cols(cb)])

    @pl.when(j == pl.num_programs(1) - 1)
    def _emit():
        for cb in range(n_cb):
            acc = acc_ref[cb]
            out = acc[:t_new]
            for hh in range(1, cb_heads):
                out = jnp.where(lane // hd == hh, acc[hh * t_new:(hh + 1) * t_new], out)
            o_ref[:, cols(cb)] = out


def _sb_sample(page_table, q, k_new, v_new, cache_kt, cache_vt, layer, bias_rows, w, *, n_heads, hd):
    db, t_new, d = q.shape
    n_pages = page_table.shape[1]
    page = cache_kt.shape[-1]
    rows = n_heads * t_new
    n_slots = min(SAMPLE_PAGES_PER_STEP, n_pages)
    seq = pl.BlockSpec((None, t_new, d), lambda b, j, pt: (b, 0, 0))

    def cache(slot):
        return pl.BlockSpec(
            (None, None, d, page),
            lambda b, j, pt: (layer, pt[b, n_pages - 1 - (j * n_slots + slot)], 0, 0))

    slots = [cache(u) for u in range(n_slots)]
    kj = lax.broadcasted_iota(jnp.int32, (n_slots * page, n_slots * page), 0)
    ks = lax.broadcasted_iota(jnp.int32, (n_slots * page, n_slots * page), 1)
    w_step = ((kj // page < ks // page) | ((kj // page == ks // page) & (kj > ks))).astype(BF16)
    return pl.pallas_call(
        functools.partial(_sb_sample_kernel, n_heads=n_heads, hd=hd, t_new=t_new, n_slots=n_slots),
        grid_spec=pltpu.PrefetchScalarGridSpec(
            num_scalar_prefetch=1,
            grid=(db, n_pages // n_slots),
            in_specs=[seq, seq, seq] + slots + slots + [
                _resident(w.shape), _resident(w_step.shape), _resident(bias_rows.shape),
            ],
            out_specs=seq,
            scratch_shapes=[
                pltpu.VMEM((d // V7X_LANES, rows * V7X_LANES // d, V7X_LANES), BF16),
                pltpu.VMEM((rows, 1), F32),
                pltpu.VMEM((d // V7X_LANES, rows * V7X_LANES // d, V7X_LANES), F32),
            ],
        ),
        out_shape=jax.ShapeDtypeStruct((db, t_new, d), F32),
        compiler_params=_params(2),
        name="sb_sample",
    )(page_table, q, k_new, v_new, *[cache_kt] * n_slots, *[cache_vt] * n_slots, w, w_step, bias_rows)


def _suffix_sum_matrix(kb):
    j = lax.broadcasted_iota(jnp.int32, (kb, kb), 0)
    s = lax.broadcasted_iota(jnp.int32, (kb, kb), 1)
    return (j > s).astype(BF16)


def kernel(x_prompt, x_sample, cache_k, cache_v, state_pool, state_conv, page_table, norm_mix, norm_mlp, norm_final, w_in_mix, w_out_mix, w_pool_grp, pool_scale, conv_w, conv_b, conv_ln_g, conv_ln_b, w_qkv, w_o, sb_bias, w_up, w_down):
    b, s, d = x_prompt.shape
    db, t_new, _ = x_sample.shape
    depth = norm_mix.shape[0]
    n_sb, n_pool_pages, page, n_heads, hd = cache_k.shape
    past_len = page_table.shape[1] * page
    pool_w = state_pool.shape[-1]
    conv_width = state_conv.shape[-1]
    n_pool_hist = state_pool.shape[2]
    n_conv_hist = state_conv.shape[2]
    q_scale = float(hd) ** -0.5 * LOG2E
    bias2 = sb_bias * LOG2E

    row = lambda x: x.reshape(1, -1)
    hp = x_prompt.reshape(b * s, d)
    hs = x_sample.transpose(1, 0, 2).reshape(t_new * db, d)
    cache_kt = cache_k.transpose(0, 1, 3, 4, 2).reshape(n_sb, n_pool_pages, d, page)
    cache_vt = cache_v.transpose(0, 1, 3, 4, 2).reshape(n_sb, n_pool_pages, d, page)
    state_pool_t = state_pool.transpose(0, 2, 1, 3)
    state_conv_t = state_conv.transpose(0, 2, 1, 3)
    w_sums_prompt = _suffix_sum_matrix(min(ATTN_K_BLOCK, s // 2))
    w_sums_sample = _suffix_sum_matrix(page)
    g_final = row(norm_final)
    w_up, w_down, w_in_mix, w_out_mix, w_o = (w.astype(BF16) for w in (w_up, w_down, w_in_mix, w_out_mix, w_o))

    kt_p = vt_p = kt_s = vt_s = None
    pool_p, pool_s, conv_p, conv_s = [], [], [], []
    for l in range(depth):
        final = l == depth - 1
        g_mix = row(norm_mix[l])
        mlp_w = (row(norm_mlp[l]), w_up, w_down, l, g_final, final)
        if l % 2 == 0:
            m = l // 2
            core_w = (w_pool_grp[m].astype(BF16), row(pool_scale[m]), conv_w[m], row(conv_b[m]),
                      row(conv_ln_g[m]), row(conv_ln_b[m]))

            ug_p = _mix_in(hp, g_mix, w_in_mix, m).reshape(b, s, pool_w + conv_width)
            ug_s = _mix_in(hs, g_mix, w_in_mix, m).reshape(t_new, db, pool_w + conv_width)
            cat_p = _mix_core(ug_p, jnp.zeros((b, n_pool_hist, pool_w), F32),
                              jnp.zeros((b, n_conv_hist, conv_width), F32), *core_w,
                              time_axis=1, nb=1, tt=min(TOKEN_TILE, s), pos0=0)
            cat_s = _mix_core(ug_s, state_pool_t[m], state_conv_t[m], *core_w,
                              time_axis=0, nb=min(SAMPLE_SEQS, db), tt=t_new, pos0=past_len)
            pool_p.append(ug_p[:, s - n_pool_hist:, :pool_w])
            conv_p.append(ug_p[:, s - n_conv_hist:, pool_w:])
            pool_s.append(jnp.concatenate([state_pool_t[m], ug_s[:, :, :pool_w]], axis=0)[-n_pool_hist:])
            conv_s.append(jnp.concatenate([state_conv_t[m], ug_s[:, :, pool_w:]], axis=0)[-n_conv_hist:])
            hp = _post(hp, cat_p.reshape(b * s, -1), w_out_mix, m, *mlp_w)
            hs = _post(hs, cat_s.reshape(t_new * db, -1), w_out_mix, m, *mlp_w)
        else:
            a = l // 2
            wq, wk, wv = (w_qkv[a][:, i * d:(i + 1) * d].astype(BF16) for i in range(3))
            qkv_w = (wq, wk, wv, wk.T, wv.T, q_scale)

            q, _, vb, ktb, kt_p, vt_p = _qkv(hp, g_mix, *qkv_w, kt_p, vt_p, a, n_sb, b)
            o_p = _sb_prompt(q.reshape(b, s, d), ktb, vb.reshape(b, s, d), bias2[a], w_sums_prompt, hd=hd)
            hp = _post(hp, o_p.reshape(b * s, d), w_o, a, *mlp_w)

            q, kb, vb, _, kt_s, vt_s = _qkv(hs, g_mix, *qkv_w, kt_s, vt_s, a, n_sb, t_new)
            by_seq = lambda x: x.reshape(t_new, db, d).transpose(1, 0, 2).astype(F32)
            bias_rows = jnp.broadcast_to(jnp.repeat(bias2[a], t_new)[:, None], (n_heads * t_new, page))
            o_s = _sb_sample(page_table, by_seq(q), by_seq(kb), by_seq(vb), cache_kt, cache_vt, a,
                             bias_rows, w_sums_sample, n_heads=n_heads, hd=hd)
            hs = _post(hs, o_s.transpose(1, 0, 2).reshape(t_new * db, d), w_o, a, *mlp_w)

    kv_prompt = lambda x: x.reshape(n_sb, b, n_heads, hd, s).transpose(0, 1, 4, 2, 3)
    kv_sample = lambda x: x.reshape(n_sb, t_new, n_heads, hd, db).transpose(0, 4, 1, 2, 3)
    state_sample = lambda xs: jnp.stack(xs).transpose(0, 2, 1, 3)
    return (hp.reshape(b, s, d), hs.reshape(t_new, db, d).transpose(1, 0, 2),
            kv_prompt(kt_p), kv_prompt(vt_p), kv_sample(kt_s), kv_sample(vt_s),
            jnp.stack(pool_p), state_sample(pool_s), jnp.stack(conv_p), state_sample(conv_s))
```

```python
import functools

import jax
import jax.numpy as jnp
from jax import lax
from jax.experimental import pallas as pl
from jax.experimental.pallas import tpu as pltpu

F32 = jnp.float32
BF16 = jnp.bfloat16

EPS = 1e-6
POOL_WINDOWS = (2, 4, 8, 16)

V7X_SUBLANES = 8
V7X_LANES = 128
V7X_VMEM_BYTES = 64 * 1024 * 1024
VMEM_LIMIT_BYTES = V7X_VMEM_BYTES * 7 // 8

TOKEN_TILE = 512
FUSED_TOKEN_TILE = 256
MLP_CHUNK = 1024
MIX_CHUNK_VREGS = 16
ATTN_K_BLOCK = 256
ATTN_COL_BLOCKS = 4
SAMPLE_PAGES_PER_STEP = 8
LOG2E = 1.4426950408889634
POOL_HALO = 16
SAMPLE_SEQS = 32


def _params(n_axes):
    return pltpu.CompilerParams(
        dimension_semantics=("arbitrary",) * n_axes,
        vmem_limit_bytes=VMEM_LIMIT_BYTES,
    )


def _resident(shape):
    zeros = (0,) * len(shape)
    return pl.BlockSpec(shape, lambda *_: zeros, pipeline_mode=pl.Buffered(1))


def _layer_resident(stack, layer):
    tail = (0,) * (stack.ndim - 1)
    return pl.BlockSpec((None,) + stack.shape[1:], lambda *_: (layer,) + tail, pipeline_mode=pl.Buffered(1))


def _rmsnorm(x, g):
    return x * lax.rsqrt(jnp.mean(x * x, axis=-1, keepdims=True) + EPS) * g


def _sigmoid(x):
    return 1.0 / (1.0 + jnp.exp(-x))


def _dot(a, b):
    return jnp.dot(a, b, preferred_element_type=F32)


def _dot_nt(a, b):
    return lax.dot_general(a, b, (((1,), (1,)), ((), ())), preferred_element_type=F32)


def _mix_in_kernel(x_ref, g_ref, w_ref, o_ref, *, pool_w, conv_w):
    xn = _rmsnorm(x_ref[...], g_ref[...]).astype(BF16)
    o_ref[:, :pool_w] = _dot(xn, w_ref[:, :pool_w])
    a = _dot(xn, w_ref[:, pool_w:pool_w + conv_w])
    gate = _dot(xn, w_ref[:, pool_w + conv_w:])
    o_ref[:, pool_w:] = a * _sigmoid(gate)


def _mix_in(h, g, w, layer):
    n, d = h.shape
    cols = w.shape[2]
    pool_w = conv_w = cols // 3
    tm = min(TOKEN_TILE, n)
    return pl.pallas_call(
        functools.partial(_mix_in_kernel, pool_w=pool_w, conv_w=conv_w),
        grid=(n // tm,),
        in_specs=[
            pl.BlockSpec((tm, d), lambda i: (i, 0)),
            _resident((1, d)),
            _layer_resident(w, layer),
        ],
        out_specs=pl.BlockSpec((tm, pool_w + conv_w), lambda i: (i, 0)),
        out_shape=jax.ShapeDtypeStruct((n, pool_w + conv_w), F32),
        compiler_params=_params(1),
        name="mix_in",
    )(h, g, w)


def _qkv_kernel(x_ref, g_ref, wq_ref, wk_ref, wv_ref, wkt_ref, wvt_ref, *refs, q_scale, has_alias):
    q_ref, kb_ref, vb_ref, ktb_ref, kt_ref, vt_ref = refs[2 if has_alias else 0:]
    xn = _rmsnorm(x_ref[...], g_ref[...]).astype(BF16)
    q_ref[...] = (_dot(xn, wq_ref[...]) * q_scale).astype(BF16)
    kb_ref[...] = _dot(xn, wk_ref[...]).astype(BF16)
    vb_ref[...] = _dot(xn, wv_ref[...]).astype(BF16)
    kt = _dot_nt(wkt_ref[...], xn)
    kt_ref[...] = kt
    ktb_ref[...] = kt.astype(BF16)
    vt_ref[...] = _dot_nt(wvt_ref[...], xn)


def _qkv(h, g, wq, wk, wv, wkt, wvt, q_scale, kt_all, vt_all, layer, n_layers, groups):
    n, d = h.shape
    glen = n // groups
    tm = min(TOKEN_TILE, glen)
    per_group = glen // tm
    tile = pl.BlockSpec((tm, d), lambda i: (i, 0))
    t_tile = pl.BlockSpec((None, None, d, tm), lambda i: (layer, i // per_group, 0, i % per_group))
    tb_tile = pl.BlockSpec((None, d, tm), lambda i: (i // per_group, 0, i % per_group))
    weights = [_resident((d, d))] * 5
    has_alias = kt_all is not None
    alias_args = (kt_all, vt_all) if has_alias else ()
    alias_specs = [pl.BlockSpec(memory_space=pl.ANY)] * len(alias_args)
    t_shape = jax.ShapeDtypeStruct((n_layers, groups, d, glen), F32)
    return pl.pallas_call(
        functools.partial(_qkv_kernel, q_scale=q_scale, has_alias=has_alias),
        grid=(n // tm,),
        in_specs=[tile, _resident((1, d))] + weights + alias_specs,
        out_specs=[tile, tile, tile, tb_tile, t_tile, t_tile],
        out_shape=[jax.ShapeDtypeStruct((n, d), BF16)] * 3
        + [jax.ShapeDtypeStruct((groups, d, glen), BF16), t_shape, t_shape],
        input_output_aliases={7: 4, 8: 5} if has_alias else {},
        compiler_params=_params(1),
        name="qkv",
    )(h, g, wq, wk, wv, wkt, wvt, *alias_args)


def _post_kernel(h_ref, a_ref, wo_ref, g_ref, wup_ref, wdn_ref, gf_ref, o_ref, *, n_chunks, final):
    h1 = h_ref[...] + _dot(a_ref[...].astype(BF16), wo_ref[...])
    xn = _rmsnorm(h1, g_ref[...]).astype(BF16)
    ck = wup_ref.shape[1] // n_chunks
    acc = h1
    for c in range(n_chunks):
        up = _dot(xn, wup_ref[:, c * ck:(c + 1) * ck])
        act = jnp.square(jnp.maximum(up, 0.0)).astype(BF16)
        acc = acc + _dot(act, wdn_ref[c * ck:(c + 1) * ck, :])
    if final:
        acc = _rmsnorm(acc, gf_ref[...])
    o_ref[...] = acc


def _post(h, a, wo, wo_layer, g, wup, wdn, mlp_layer, g_final, final):
    n, d = h.shape
    dff = wup.shape[2]
    tm = min(TOKEN_TILE, n)
    tile = pl.BlockSpec((tm, d), lambda i: (i, 0))
    return pl.pallas_call(
        functools.partial(_post_kernel, n_chunks=max(1, dff // MLP_CHUNK), final=final),
        grid=(n // tm,),
        in_specs=[
            tile,
            pl.BlockSpec((tm, a.shape[1]), lambda i: (i, 0)),
            _layer_resident(wo, wo_layer),
            _resident((1, d)),
            _layer_resident(wup, mlp_layer),
            _layer_resident(wdn, mlp_layer),
            _resident((1, d)),
        ],
        out_specs=tile,
        out_shape=jax.ShapeDtypeStruct((n, d), F32),
        compiler_params=_params(1),
        name="post_mlp",
    )(h, a, wo, g, wup, wdn, g_final)


def _mix_core_kernel(ug_ref, hp_ref, hc_ref, wg_ref, sc_ref, cw_ref, cb_ref, lg_ref, lb_ref,
                     o_ref, uf_ref, gf_ref, d_ref, *gs_refs, time_axis, nb, tt, pos0, n_t, conv_halo):
    ti = pl.program_id(1)
    pool_w = hp_ref.shape[-1]
    conv_w = hc_ref.shape[-1]
    n_pool_hist = hp_ref.shape[time_axis]
    n_conv_hist = hc_ref.shape[time_axis]
    taps = cw_ref.shape[0]
    group_w = pool_w // len(POOL_WINDOWS)

    def rows(start, size, lanes=slice(None)):
        t = slice(start, start + size)
        return (t, slice(None), lanes) if time_axis == 0 else (slice(None), t, lanes)

    def shape(n_rows, width):
        return (n_rows, nb, width) if time_axis == 0 else (nb, n_rows, width)

    if time_axis == 0:
        rc = max(1, MIX_CHUNK_VREGS * V7X_SUBLANES * V7X_LANES // (nb * conv_w))
    else:
        rc = MIX_CHUNK_VREGS * V7X_SUBLANES * V7X_LANES // conv_w
    rc = min(rc, tt)
    row_iota_shape = (rc, 1, 1) if time_axis == 0 else (1, rc, 1)

    @pl.when(ti == 0)
    def _load_history():
        uf_ref[rows(0, POOL_HALO - n_pool_hist)] = jnp.zeros(shape(POOL_HALO - n_pool_hist, pool_w), F32)
        uf_ref[rows(POOL_HALO - n_pool_hist, n_pool_hist)] = hp_ref[...]
        gf_ref[rows(0, conv_halo - n_conv_hist)] = jnp.zeros(shape(conv_halo - n_conv_hist, conv_w), F32)
        gf_ref[rows(conv_halo - n_conv_hist, n_conv_hist)] = hc_ref[...]

    uf_ref[rows(POOL_HALO, tt)] = ug_ref[:, :, :pool_w]
    gf_ref[rows(conv_halo, tt)] = ug_ref[:, :, pool_w:]

    if time_axis == 1:
        gs_ref, = gs_refs
        n_rows = conv_halo + tt
        for sft in range(1, V7X_SUBLANES):
            gs_ref[sft - 1, :, 0:n_rows - sft, :] = gf_ref[:, sft:n_rows, :]

    def conv_rows(start):
        sft = start % V7X_SUBLANES
        if time_axis == 0 or sft == 0:
            return gf_ref[rows(start, rc)]
        return gs_ref[sft - 1, :, start - sft:start - sft + rc, :]

    cb = cb_ref[...].reshape(1, 1, conv_w)
    lg = lg_ref[...].reshape(1, 1, conv_w)
    lb = lb_ref[...].reshape(1, 1, conv_w)

    for r0 in range(0, tt, rc):
        pos = pos0 + ti * tt + r0 + lax.broadcasted_iota(jnp.int32, row_iota_shape, time_axis)
        for gi, win in enumerate(POOL_WINDOWS):
            lanes = slice(gi * group_w, (gi + 1) * group_w)
            u_new = uf_ref[rows(POOL_HALO + r0, rc, lanes)]
            s = u_new
            for i in range(1, win):
                s = s + uf_ref[rows(POOL_HALO + r0 - i, rc, lanes)]
            cnt = jnp.minimum(win, pos + 1).astype(F32)
            d_ref[rows(r0, rc, lanes)] = s / cnt - u_new

        base = conv_halo - n_conv_hist + r0
        acc = jnp.zeros(shape(rc, conv_w), F32)
        for k in range(taps):
            acc = acc + conv_rows(base + k) * cw_ref[k:k + 1, :].reshape(1, 1, conv_w)
        y = acc + cb
        mu = jnp.mean(y, axis=-1, keepdims=True)
        yc = y - mu
        var = jnp.mean(yc * yc, axis=-1, keepdims=True)
        yn = yc * lax.rsqrt(var + EPS) * lg + lb
        o_ref[rows(r0, rc, slice(pool_w, pool_w + conv_w))] = yn * _sigmoid(yn)

    for gi in range(len(POOL_WINDOWS)):
        lanes = slice(gi * group_w, (gi + 1) * group_w)
        d = d_ref[:, :, lanes].reshape(nb * tt, group_w).astype(BF16)
        y = _dot(d, wg_ref[gi]) * sc_ref[:, lanes]
        o_ref[:, :, lanes] = y.reshape(shape(tt, group_w))

    if n_t > 1:
        uf_ref[rows(0, POOL_HALO)] = uf_ref[rows(tt, POOL_HALO)]
        gf_ref[rows(0, conv_halo)] = gf_ref[rows(tt, conv_halo)]


def _mix_core(ug, hist_pool, hist_conv, w_grp, scale, conv_w, conv_b, ln_g, ln_b, *,
              time_axis, nb, tt, pos0):
    seq_axis = 1 - time_axis
    n_seq, t, width = ug.shape[seq_axis], ug.shape[time_axis], ug.shape[2]
    pool_w = hist_pool.shape[-1]
    cw = hist_conv.shape[-1]
    n_t = t // tt
    conv_halo = -(-hist_conv.shape[time_axis] // V7X_SUBLANES) * V7X_SUBLANES

    def block(n_rows, c):
        if time_axis == 0:
            return pl.BlockSpec((n_rows, nb, c), lambda b, i: (i if n_rows == tt else 0, b, 0))
        return pl.BlockSpec((nb, n_rows, c), lambda b, i: (b, i if n_rows == tt else 0, 0))

    def scratch(n_rows, c):
        return pltpu.VMEM((n_rows, nb, c) if time_axis == 0 else (nb, n_rows, c), F32)

    return pl.pallas_call(
        functools.partial(_mix_core_kernel, time_axis=time_axis, nb=nb, tt=tt, pos0=pos0, n_t=n_t,
                          conv_halo=conv_halo),
        grid=(n_seq // nb, n_t),
        in_specs=[
            block(tt, width),
            block(hist_pool.shape[time_axis], pool_w),
            block(hist_conv.shape[time_axis], cw),
            _resident(w_grp.shape),
            _resident(scale.shape),
            _resident(conv_w.shape),
            _resident(conv_b.shape),
            _resident(ln_g.shape),
            _resident(ln_b.shape),
        ],
        out_specs=block(tt, width),
        out_shape=jax.ShapeDtypeStruct(ug.shape, F32),
        scratch_shapes=[scratch(POOL_HALO + tt, pool_w), scratch(conv_halo + tt, cw), scratch(tt, pool_w)]
        + ([pltpu.VMEM((V7X_SUBLANES - 1, nb, conv_halo + tt, cw), F32)] if time_axis == 1 else []),
        compiler_params=_params(2),
        name="mix_core",
    )(ug, hist_pool, hist_conv, w_grp, scale, conv_w, conv_b, ln_g, ln_b)


def _sb_scores(z, mask):
    sign = jnp.uint32(0x80000000)
    neg_abs = lax.bitcast_convert_type(lax.bitcast_convert_type(z, jnp.uint32) | sign, F32)
    sp = jnp.maximum(z, 0.0) + jnp.log(1.0 + jnp.exp2(neg_abs)) * LOG2E
    log_beta = z - sp
    if mask is not None:
        sp = jnp.where(mask, sp, 0.0)
        log_beta = jnp.where(mask, log_beta, -jnp.inf)
    return sp.astype(BF16), log_beta, sp[:, :1]


def _sb_apply(scores, carry, w, first=0):
    sp, log_beta, sp_first = scores
    later = _dot(sp, w) + carry
    return later[:, first:first + 1] + sp_first, jnp.exp2(log_beta - later)


def _sb_prompt_kernel(bias_ref, q_ref, kt_ref, v_ref, w_ref, o_ref, sp_ref, lb_ref, carry_ref, acc_ref,
                      *, tq, kb, hd):
    qi = pl.program_id(2)
    width = V7X_LANES
    n_cb = q_ref.shape[-1] // width
    heads = width // hd
    lane = lax.broadcasted_iota(jnp.int32, (tq, width), 1)
    tri = lax.broadcasted_iota(jnp.int32, (kb, kb), 1) < lax.broadcasted_iota(jnp.int32, (kb, kb), 0)
    w = w_ref[...]
    in_head = [(lane >= hh * hd) & (lane < (hh + 1) * hd) for hh in range(heads)]
    one_lanes = jnp.where(lane < 2, 1.0, 0.0).astype(BF16)
    krow = lax.broadcasted_iota(jnp.int32, (width, kb), 0)
    groups = []
    for cb in range(n_cb):
        q = q_ref[:, cb * width:(cb + 1) * width].astype(F32)
        for hh in range(heads):
            qa = jnp.concatenate([jnp.where(in_head[hh], q, 0.0).astype(BF16), one_lanes], axis=1)
            b = jnp.full((width, kb), bias_ref[(pl.program_id(1) * n_cb + cb) * heads + hh], F32)
            b_hi = b.astype(BF16).astype(F32)
            rows = jnp.where(krow == 0, b_hi, jnp.where(krow == 1, b - b_hi, 0.0)).astype(BF16)
            groups.append((cb, qa, rows))

    def score(slot, blk, r0, r1, mask):
        start = pl.multiple_of(blk * kb, kb)
        for g, (cb, qa, bias_rows) in enumerate(groups):
            kt = kt_ref[cb * width:(cb + 1) * width, pl.ds(start, kb)]
            z = _dot(qa[r0:r1], jnp.concatenate([kt, bias_rows], axis=0))
            sp, log_beta, _ = _sb_scores(z, mask)
            sp_ref[slot, g, r0:r1] = sp
            lb_ref[slot, g, r0:r1] = log_beta

    def accumulate(slot, blk, r0, r1):
        start = pl.multiple_of(blk * kb, kb)
        for g, (cb, _, _) in enumerate(groups):
            sp = sp_ref[slot, g, r0:r1]
            scores = (sp, lb_ref[slot, g, r0:r1], sp[:, :1].astype(F32))
            carry_ref[g, r0:r1], a = _sb_apply(scores, carry_ref[g, r0:r1], w)
            acc_ref[g, r0:r1] += _dot(a.astype(BF16), v_ref[pl.ds(start, kb), cb * width:(cb + 1) * width])

    first = 2 * qi
    carry_ref[...] = jnp.zeros(carry_ref.shape, F32)
    acc_ref[...] = jnp.zeros(acc_ref.shape, F32)
    score(0, first + 1, kb, tq, tri)
    score(1, first, 0, kb, tri)
    score(1, first, kb, tq, None)
    accumulate(0, first + 1, kb, tq)

    @pl.loop(0, qi)
    def _(i):
        blk = first - 1 - 2 * i
        score(0, blk, 0, tq, None)
        accumulate(1, blk + 1, 0, tq)
        score(1, blk - 1, 0, tq, None)
        accumulate(0, blk, 0, tq)

    accumulate(1, 0, 0, tq)
    for cb in range(n_cb):
        out = acc_ref[cb * heads]
        for hh in range(1, heads):
            out = jnp.where(in_head[hh], acc_ref[cb * heads + hh], out)
        o_ref[:, cb * width:(cb + 1) * width] = out


def _sb_prompt(q, kt, v, bias, w, *, hd):
    b, s, d = q.shape
    kb = w.shape[1]
    tq = 2 * kb
    cols = ATTN_COL_BLOCKS * V7X_LANES
    n_groups = cols // hd
    qtile = pl.BlockSpec((None, tq, cols), lambda bi, p, i, *_: (bi, i, p))
    return pl.pallas_call(
        functools.partial(_sb_prompt_kernel, tq=tq, kb=kb, hd=hd),
        grid_spec=pltpu.PrefetchScalarGridSpec(
            num_scalar_prefetch=1,
            grid=(b, d // cols, s // tq),
            in_specs=[
                qtile,
                pl.BlockSpec((None, cols, s), lambda bi, p, i, *_: (bi, p, 0)),
                pl.BlockSpec((None, s, cols), lambda bi, p, i, *_: (bi, 0, p)),
                _resident(w.shape),
            ],
            out_specs=qtile,
            scratch_shapes=[
                pltpu.VMEM((2, n_groups, tq, kb), BF16),
                pltpu.VMEM((2, n_groups, tq, kb), F32),
                pltpu.VMEM((n_groups, tq, 1), F32),
                pltpu.VMEM((n_groups, tq, V7X_LANES), F32),
            ],
        ),
        out_shape=jax.ShapeDtypeStruct((b, s, d), F32),
        compiler_params=_params(3),
        name="sb_prompt",
    )(bias, q, kt, v, w)


def _sb_sample_kernel(pt_ref, q_ref, kn_ref, vn_ref, *refs, n_heads, hd, t_new, n_slots):
    kc_refs, vc_refs = refs[:n_slots], refs[n_slots:2 * n_slots]
    w_ref, wall_ref, b_ref, o_ref, qcb_ref, carry_ref, acc_ref = refs[2 * n_slots:]
    j = pl.program_id(1)
    rows = n_heads * t_new
    page = kc_refs[0].shape[-1]
    d = q_ref.shape[-1]
    width = V7X_LANES
    n_cb = d // width
    cb_heads = width // hd
    cb_rows = cb_heads * t_new
    bias = b_ref[...]
    lane = lax.broadcasted_iota(jnp.int32, (t_new, width), 1)

    def cols(cb):
        return slice(cb * width, (cb + 1) * width)

    def head_rows(cb):
        return slice(cb * cb_rows, (cb + 1) * cb_rows)

    @pl.when(j == 0)
    def _new_keys():
        for cb in range(n_cb):
            q = q_ref[:, cols(cb)]
            qcb_ref[cb] = jnp.concatenate(
                [jnp.where(lane // hd == hh, q, 0.0) for hh in range(cb_heads)], axis=0).astype(BF16)
        pad = jnp.zeros((page - t_new, d), F32)
        kn = jnp.concatenate([kn_ref[...], pad], axis=0).astype(BF16)
        vn = jnp.concatenate([vn_ref[...], pad], axis=0).astype(BF16)
        rr = lax.broadcasted_iota(jnp.int32, (rows, page), 0)
        cc = lax.broadcasted_iota(jnp.int32, (rows, page), 1)
        z = jnp.concatenate([_dot_nt(qcb_ref[cb], kn[:, cols(cb)]) for cb in range(n_cb)], axis=0) + bias
        carry, a = _sb_apply(_sb_scores(z, cc < rr % t_new), jnp.zeros((rows, 1), F32), w_ref[...])
        carry_ref[...] = carry
        a = a.astype(BF16)
        for cb in range(n_cb):
            acc_ref[cb] = _dot(a[head_rows(cb)], vn[:, cols(cb)])

    kt = jnp.concatenate([kc_ref[...].astype(BF16) for kc_ref in kc_refs], axis=1)
    vt = jnp.concatenate([vc_ref[...].astype(BF16) for vc_ref in vc_refs], axis=1)
    z = jnp.concatenate([_dot(qcb_ref[cb], kt[cols(cb)]) for cb in range(n_cb)], axis=0)
    sp, log_beta, _ = _sb_scores(z + jnp.concatenate([bias] * n_slots, axis=1), None)
    tile = 2 * page
    later = jnp.concatenate(
        [_dot(sp[:, :(c + 1) * tile], wall_ref[:(c + 1) * tile, c * tile:(c + 1) * tile])
         for c in range(n_slots * page // tile)], axis=1) + carry_ref[...]
    a = jnp.exp2(log_beta - later).astype(BF16)
    oldest = (n_slots - 1) * page
    carry_ref[...] = later[:, oldest:oldest + 1] + sp[:, oldest:oldest + 1].astype(F32)
    for cb in range(n_cb):
        acc_ref[cb] += _dot_nt(a[head_rows(cb)], vt[cols(cb)])

    @pl.when(j == pl.num_programs(1) - 1)
    def _emit():
        for cb in range(n_cb):
            acc = acc_ref[cb]
            out = acc[:t_new]
            for hh in range(1, cb_heads):
                out = jnp.where(lane // hd == hh, acc[hh * t_new:(hh + 1) * t_new], out)
            o_ref[:, cols(cb)] = out


def _sb_sample_post_kernel(pt_ref, *refs, n_slots, steps_per_tile, sample_kw, post_kw):
    n_sample_in = 3 + 2 * n_slots + 3
    sample_in, post_in = refs[:n_sample_in], refs[n_sample_in:n_sample_in + 7]
    o_sample_ref, o_post_ref = refs[n_sample_in + 7:n_sample_in + 9]
    scratch = refs[n_sample_in + 9:]
    _sb_sample_kernel(pt_ref, *sample_in, o_sample_ref, *scratch[:3], n_slots=n_slots, **sample_kw)

    h_ref, a_ref, wo_ref, g_ref, wup_ref, wdn_ref, gf_ref = post_in
    xn_ref, h1_ref = scratch[3:]
    step = pl.program_id(0) * pl.num_programs(1) + pl.program_id(1)
    phase = step % steps_per_tile
    ck = wup_ref.shape[1] // steps_per_tile

    @pl.when(phase == 0)
    def _project():
        h1 = h_ref[...] + _dot(a_ref[...].astype(BF16), wo_ref[...])
        h1_ref[...] = h1
        xn_ref[...] = _rmsnorm(h1, g_ref[...]).astype(BF16)

    for c in range(steps_per_tile):
        @pl.when(phase == c)
        def _mlp_chunk():
            up = _dot(xn_ref[...], wup_ref[:, c * ck:(c + 1) * ck])
            act = jnp.square(jnp.maximum(up, 0.0)).astype(BF16)
            h1_ref[...] += _dot(act, wdn_ref[c * ck:(c + 1) * ck, :])

    @pl.when(phase == steps_per_tile - 1)
    def _emit_tile():
        out = h1_ref[...]
        o_post_ref[...] = _rmsnorm(out, gf_ref[...]) if post_kw["final"] else out


def _sb_sample_post(page_table, q, k_new, v_new, cache_kt, cache_vt, layer, bias_rows, w, post_args, *,
                    n_heads, hd):
    h, a, wo, wo_layer, g, wup, wdn, mlp_layer, g_final, final = post_args
    db, t_new, d = q.shape
    n_pages = page_table.shape[1]
    page = cache_kt.shape[-1]
    rows = n_heads * t_new
    n_slots = min(SAMPLE_PAGES_PER_STEP, n_pages)
    n_j = n_pages // n_slots
    n_tok = h.shape[0]
    tm = min(FUSED_TOKEN_TILE, n_tok)
    steps_per_tile = db * n_j * tm // n_tok
    assert steps_per_tile >= 1 and steps_per_tile * n_tok == db * n_j * tm
    seq = pl.BlockSpec((None, t_new, d), lambda b, j, pt: (b, 0, 0))
    tok = lambda width: pl.BlockSpec((tm, width), lambda b, j, pt: ((b * n_j + j) // steps_per_tile, 0))

    def cache(slot):
        return pl.BlockSpec(
            (None, None, d, page),
            lambda b, j, pt: (layer, pt[b, n_pages - 1 - (j * n_slots + slot)], 0, 0))

    slots = [cache(u) for u in range(n_slots)]
    kj = lax.broadcasted_iota(jnp.int32, (n_slots * page, n_slots * page), 0)
    ks = lax.broadcasted_iota(jnp.int32, (n_slots * page, n_slots * page), 1)
    w_step = ((kj // page < ks // page) | ((kj // page == ks // page) & (kj > ks))).astype(BF16)
    return pl.pallas_call(
        functools.partial(
            _sb_sample_post_kernel, n_slots=n_slots, steps_per_tile=steps_per_tile,
            sample_kw=dict(n_heads=n_heads, hd=hd, t_new=t_new),
            post_kw=dict(final=final)),
        grid_spec=pltpu.PrefetchScalarGridSpec(
            num_scalar_prefetch=1,
            grid=(db, n_j),
            in_specs=[seq, seq, seq] + slots + slots + [
                _resident(w.shape), _resident(w_step.shape), _resident(bias_rows.shape),
                tok(d), tok(a.shape[1]), _layer_resident(wo, wo_layer), _resident((1, d)),
                _layer_resident(wup, mlp_layer), _layer_resident(wdn, mlp_layer), _resident((1, d)),
            ],
            out_specs=[seq, tok(d)],
            scratch_shapes=[
                pltpu.VMEM((d // V7X_LANES, rows * V7X_LANES // d, V7X_LANES), BF16),
                pltpu.VMEM((rows, 1), F32),
                pltpu.VMEM((d // V7X_LANES, rows * V7X_LANES // d, V7X_LANES), F32),
                pltpu.VMEM((tm, d), BF16),
                pltpu.VMEM((tm, d), F32),
            ],
        ),
        out_shape=[jax.ShapeDtypeStruct((db, t_new, d), F32), jax.ShapeDtypeStruct((n_tok, d), F32)],
        compiler_params=_params(2),
        name="sb_sample_post",
    )(page_table, q, k_new, v_new, *[cache_kt] * n_slots, *[cache_vt] * n_slots, w, w_step, bias_rows,
      h, a, wo, g, wup, wdn, g_final)


def _suffix_sum_matrix(kb):
    j = lax.broadcasted_iota(jnp.int32, (kb, kb), 0)
    s = lax.broadcasted_iota(jnp.int32, (kb, kb), 1)
    return (j > s).astype(BF16)


def kernel(x_prompt, x_sample, cache_k, cache_v, state_pool, state_conv, page_table, norm_mix, norm_mlp, norm_final, w_in_mix, w_out_mix, w_pool_grp, pool_scale, conv_w, conv_b, conv_ln_g, conv_ln_b, w_qkv, w_o, sb_bias, w_up, w_down):
    b, s, d = x_prompt.shape
    db, t_new, _ = x_sample.shape
    depth = norm_mix.shape[0]
    n_sb, n_pool_pages, page, n_heads, hd = cache_k.shape
    past_len = page_table.shape[1] * page
    pool_w = state_pool.shape[-1]
    conv_width = state_conv.shape[-1]
    n_pool_hist = state_pool.shape[2]
    n_conv_hist = state_conv.shape[2]
    q_scale = float(hd) ** -0.5 * LOG2E
    bias2 = sb_bias * LOG2E

    row = lambda x: x.reshape(1, -1)
    hp = x_prompt.reshape(b * s, d)
    hs = x_sample.transpose(1, 0, 2).reshape(t_new * db, d)
    cache_kt = cache_k.transpose(0, 1, 3, 4, 2).reshape(n_sb, n_pool_pages, d, page)
    cache_vt = cache_v.transpose(0, 1, 3, 4, 2).reshape(n_sb, n_pool_pages, d, page)
    state_pool_t = state_pool.transpose(0, 2, 1, 3)
    state_conv_t = state_conv.transpose(0, 2, 1, 3)
    w_sums_prompt = _suffix_sum_matrix(min(ATTN_K_BLOCK, s // 2))
    w_sums_sample = _suffix_sum_matrix(page)
    g_final = row(norm_final)
    w_up, w_down, w_in_mix, w_out_mix, w_o = (w.astype(BF16) for w in (w_up, w_down, w_in_mix, w_out_mix, w_o))

    kt_p = vt_p = kt_s = vt_s = None
    pool_p, pool_s, conv_p, conv_s = [], [], [], []
    for l in range(depth):
        final = l == depth - 1
        g_mix = row(norm_mix[l])
        mlp_w = (row(norm_mlp[l]), w_up, w_down, l, g_final, final)
        if l % 2 == 0:
            m = l // 2
            core_w = (w_pool_grp[m].astype(BF16), row(pool_scale[m]), conv_w[m], row(conv_b[m]),
                      row(conv_ln_g[m]), row(conv_ln_b[m]))

            ug_p = _mix_in(hp, g_mix, w_in_mix, m).reshape(b, s, pool_w + conv_width)
            ug_s = _mix_in(hs, g_mix, w_in_mix, m).reshape(t_new, db, pool_w + conv_width)
            cat_p = _mix_core(ug_p, jnp.zeros((b, n_pool_hist, pool_w), F32),
                              jnp.zeros((b, n_conv_hist, conv_width), F32), *core_w,
                              time_axis=1, nb=1, tt=min(TOKEN_TILE, s), pos0=0)
            cat_s = _mix_core(ug_s, state_pool_t[m], state_conv_t[m], *core_w,
                              time_axis=0, nb=min(SAMPLE_SEQS, db), tt=t_new, pos0=past_len)
            pool_p.append(ug_p[:, s - n_pool_hist:, :pool_w])
            conv_p.append(ug_p[:, s - n_conv_hist:, pool_w:])
            pool_s.append(jnp.concatenate([state_pool_t[m], ug_s[:, :, :pool_w]], axis=0)[-n_pool_hist:])
            conv_s.append(jnp.concatenate([state_conv_t[m], ug_s[:, :, pool_w:]], axis=0)[-n_conv_hist:])
            hp = _post(hp, cat_p.reshape(b * s, -1), w_out_mix, m, *mlp_w)
            hs = _post(hs, cat_s.reshape(t_new * db, -1), w_out_mix, m, *mlp_w)
        else:
            a = l // 2
            wq, wk, wv = (w_qkv[a][:, i * d:(i + 1) * d].astype(BF16) for i in range(3))
            qkv_w = (wq, wk, wv, wk.T, wv.T, q_scale)

            q, _, vb, ktb, kt_p, vt_p = _qkv(hp, g_mix, *qkv_w, kt_p, vt_p, a, n_sb, b)
            o_p = _sb_prompt(q.reshape(b, s, d), ktb, vb.reshape(b, s, d), bias2[a], w_sums_prompt, hd=hd)

            q, kb, vb, _, kt_s, vt_s = _qkv(hs, g_mix, *qkv_w, kt_s, vt_s, a, n_sb, t_new)
            by_seq = lambda x: x.reshape(t_new, db, d).transpose(1, 0, 2).astype(F32)
            bias_rows = jnp.broadcast_to(jnp.repeat(bias2[a], t_new)[:, None], (n_heads * t_new, page))
            o_s, hp = _sb_sample_post(page_table, by_seq(q), by_seq(kb), by_seq(vb), cache_kt, cache_vt, a,
                                      bias_rows, w_sums_sample, (hp, o_p.reshape(b * s, d), w_o, a, *mlp_w),
                                      n_heads=n_heads, hd=hd)
            hs = _post(hs, o_s.transpose(1, 0, 2).reshape(t_new * db, d), w_o, a, *mlp_w)

    kv_prompt = lambda x: x.reshape(n_sb, b, n_heads, hd, s).transpose(0, 1, 4, 2, 3)
    kv_sample = lambda x: x.reshape(n_sb, t_new, n_heads, hd, db).transpose(0, 4, 1, 2, 3)
    state_sample = lambda xs: jnp.stack(xs).transpose(0, 2, 1, 3)
    return (hp.reshape(b, s, d), hs.reshape(t_new, db, d).transpose(1, 0, 2),
            kv_prompt(kt_p), kv_prompt(vt_p), kv_sample(kt_s), kv_sample(vt_s),
            jnp.stack(pool_p), state_sample(pool_s), jnp.stack(conv_p), state_sample(conv_s))
```

```python
import functools

import jax
import jax.numpy as jnp
from jax import lax
from jax.experimental import pallas as pl
from jax.experimental.pallas import tpu as pltpu

F32 = jnp.float32
BF16 = jnp.bfloat16

EPS = 1e-6
POOL_WINDOWS = (2, 4, 8, 16)

V7X_SUBLANES = 8
V7X_LANES = 128
V7X_VMEM_BYTES = 64 * 1024 * 1024
VMEM_LIMIT_BYTES = V7X_VMEM_BYTES * 7 // 8

TOKEN_TILE = 512
FUSED_TOKEN_TILE = 256
MLP_CHUNK = 1024
MIX_CHUNK_VREGS = 16
ATTN_K_BLOCK = 256
ATTN_COL_BLOCKS = 4
SAMPLE_PAGES_PER_STEP = 8
LOG2E = 1.4426950408889634
POOL_HALO = 16
SAMPLE_SEQS = 32


def _params(n_axes):
    return pltpu.CompilerParams(
        dimension_semantics=("arbitrary",) * n_axes,
        vmem_limit_bytes=VMEM_LIMIT_BYTES,
    )


def _resident(shape):
    zeros = (0,) * len(shape)
    return pl.BlockSpec(shape, lambda *_: zeros, pipeline_mode=pl.Buffered(1))


def _layer_resident(stack, layer):
    tail = (0,) * (stack.ndim - 1)
    return pl.BlockSpec((None,) + stack.shape[1:], lambda *_: (layer,) + tail, pipeline_mode=pl.Buffered(1))


def _rmsnorm(x, g):
    return x * lax.rsqrt(jnp.mean(x * x, axis=-1, keepdims=True) + EPS) * g


def _sigmoid(x):
    return 1.0 / (1.0 + jnp.exp(-x))


def _dot(a, b):
    return jnp.dot(a, b, preferred_element_type=F32)


def _dot_nt(a, b):
    return lax.dot_general(a, b, (((1,), (1,)), ((), ())), preferred_element_type=F32)


def _mix_in_kernel(x_ref, g_ref, w_ref, o_ref, *, pool_w, conv_w):
    xn = _rmsnorm(x_ref[...], g_ref[...]).astype(BF16)
    o_ref[:, :pool_w] = _dot(xn, w_ref[:, :pool_w])
    a = _dot(xn, w_ref[:, pool_w:pool_w + conv_w])
    gate = _dot(xn, w_ref[:, pool_w + conv_w:])
    o_ref[:, pool_w:] = a * _sigmoid(gate)


def _mix_in(h, g, w, layer):
    n, d = h.shape
    cols = w.shape[2]
    pool_w = conv_w = cols // 3
    tm = min(TOKEN_TILE, n)
    return pl.pallas_call(
        functools.partial(_mix_in_kernel, pool_w=pool_w, conv_w=conv_w),
        grid=(n // tm,),
        in_specs=[
            pl.BlockSpec((tm, d), lambda i: (i, 0)),
            _resident((1, d)),
            _layer_resident(w, layer),
        ],
        out_specs=pl.BlockSpec((tm, pool_w + conv_w), lambda i: (i, 0)),
        out_shape=jax.ShapeDtypeStruct((n, pool_w + conv_w), F32),
        compiler_params=_params(1),
        name="mix_in",
    )(h, g, w)


def _qkv_kernel(x_ref, g_ref, wq_ref, wk_ref, wv_ref, wkt_ref, wvt_ref, *refs, q_scale, has_alias, k_rows):
    q_ref, kx_ref, vb_ref, kt_ref, vt_ref = refs[2 if has_alias else 0:]
    xn = _rmsnorm(x_ref[...], g_ref[...]).astype(BF16)
    q_ref[...] = (_dot(xn, wq_ref[...]) * q_scale).astype(BF16)
    vb_ref[...] = _dot(xn, wv_ref[...]).astype(BF16)
    kt = _dot_nt(wkt_ref[...], xn)
    kt_ref[...] = kt
    if k_rows:
        kx_ref[...] = _dot(xn, wk_ref[...]).astype(BF16)
    else:
        kx_ref[...] = kt.astype(BF16)
    vt_ref[...] = _dot_nt(wvt_ref[...], xn)


def _qkv(h, g, wq, wk, wv, wkt, wvt, q_scale, kt_all, vt_all, layer, n_layers, groups, k_rows):
    n, d = h.shape
    glen = n // groups
    tm = min(TOKEN_TILE, glen)
    per_group = glen // tm
    tile = pl.BlockSpec((tm, d), lambda i: (i, 0))
    t_tile = pl.BlockSpec((None, None, d, tm), lambda i: (layer, i // per_group, 0, i % per_group))
    tb_tile = pl.BlockSpec((None, d, tm), lambda i: (i // per_group, 0, i % per_group))
    weights = [_resident((d, d))] * 5
    has_alias = kt_all is not None
    alias_args = (kt_all, vt_all) if has_alias else ()
    alias_specs = [pl.BlockSpec(memory_space=pl.ANY)] * len(alias_args)
    t_shape = jax.ShapeDtypeStruct((n_layers, groups, d, glen), F32)
    rows_shape = jax.ShapeDtypeStruct((n, d), BF16)
    return pl.pallas_call(
        functools.partial(_qkv_kernel, q_scale=q_scale, has_alias=has_alias, k_rows=k_rows),
        grid=(n // tm,),
        in_specs=[tile, _resident((1, d))] + weights + alias_specs,
        out_specs=[tile, tile if k_rows else tb_tile, tile, t_tile, t_tile],
        out_shape=[rows_shape, rows_shape if k_rows else jax.ShapeDtypeStruct((groups, d, glen), BF16),
                   rows_shape, t_shape, t_shape],
        input_output_aliases={7: 3, 8: 4} if has_alias else {},
        compiler_params=_params(1),
        name="qkv",
    )(h, g, wq, wk, wv, wkt, wvt, *alias_args)


def _post_kernel(h_ref, a_ref, wo_ref, g_ref, wup_ref, wdn_ref, gf_ref, o_ref, *, n_chunks, final):
    h1 = h_ref[...] + _dot(a_ref[...].astype(BF16), wo_ref[...])
    xn = _rmsnorm(h1, g_ref[...]).astype(BF16)
    ck = wup_ref.shape[1] // n_chunks
    acc = h1
    for c in range(n_chunks):
        up = _dot(xn, wup_ref[:, c * ck:(c + 1) * ck])
        act = jnp.square(jnp.maximum(up, 0.0)).astype(BF16)
        acc = acc + _dot(act, wdn_ref[c * ck:(c + 1) * ck, :])
    if final:
        acc = _rmsnorm(acc, gf_ref[...])
    o_ref[...] = acc


def _post(h, a, wo, wo_layer, g, wup, wdn, mlp_layer, g_final, final):
    n, d = h.shape
    dff = wup.shape[2]
    tm = min(TOKEN_TILE, n)
    tile = pl.BlockSpec((tm, d), lambda i: (i, 0))
    return pl.pallas_call(
        functools.partial(_post_kernel, n_chunks=max(1, dff // MLP_CHUNK), final=final),
        grid=(n // tm,),
        in_specs=[
            tile,
            pl.BlockSpec((tm, a.shape[1]), lambda i: (i, 0)),
            _layer_resident(wo, wo_layer),
            _resident((1, d)),
            _layer_resident(wup, mlp_layer),
            _layer_resident(wdn, mlp_layer),
            _resident((1, d)),
        ],
        out_specs=tile,
        out_shape=jax.ShapeDtypeStruct((n, d), F32),
        compiler_params=_params(1),
        name="post_mlp",
    )(h, a, wo, g, wup, wdn, g_final)


def _mix_core_kernel(ug_ref, hp_ref, hc_ref, wg_ref, sc_ref, cw_ref, cb_ref, lg_ref, lb_ref,
                     o_ref, uf_ref, gf_ref, d_ref, *gs_refs, time_axis, nb, tt, pos0, n_t, conv_halo):
    ti = pl.program_id(1)
    pool_w = hp_ref.shape[-1]
    conv_w = hc_ref.shape[-1]
    n_pool_hist = hp_ref.shape[time_axis]
    n_conv_hist = hc_ref.shape[time_axis]
    taps = cw_ref.shape[0]
    group_w = pool_w // len(POOL_WINDOWS)

    def rows(start, size, lanes=slice(None)):
        t = slice(start, start + size)
        return (t, slice(None), lanes) if time_axis == 0 else (slice(None), t, lanes)

    def shape(n_rows, width):
        return (n_rows, nb, width) if time_axis == 0 else (nb, n_rows, width)

    if time_axis == 0:
        rc = max(1, MIX_CHUNK_VREGS * V7X_SUBLANES * V7X_LANES // (nb * conv_w))
    else:
        rc = MIX_CHUNK_VREGS * V7X_SUBLANES * V7X_LANES // conv_w
    rc = min(rc, tt)
    row_iota_shape = (rc, 1, 1) if time_axis == 0 else (1, rc, 1)

    @pl.when(ti == 0)
    def _load_history():
        uf_ref[rows(0, POOL_HALO - n_pool_hist)] = jnp.zeros(shape(POOL_HALO - n_pool_hist, pool_w), F32)
        uf_ref[rows(POOL_HALO - n_pool_hist, n_pool_hist)] = hp_ref[...]
        gf_ref[rows(0, conv_halo - n_conv_hist)] = jnp.zeros(shape(conv_halo - n_conv_hist, conv_w), F32)
        gf_ref[rows(conv_halo - n_conv_hist, n_conv_hist)] = hc_ref[...]

    uf_ref[rows(POOL_HALO, tt)] = ug_ref[:, :, :pool_w]
    gf_ref[rows(conv_halo, tt)] = ug_ref[:, :, pool_w:]

    if time_axis == 1:
        gs_ref, = gs_refs
        n_rows = conv_halo + tt
        for sft in range(1, V7X_SUBLANES):
            gs_ref[sft - 1, :, 0:n_rows - sft, :] = gf_ref[:, sft:n_rows, :]

    def conv_rows(start):
        sft = start % V7X_SUBLANES
        if time_axis == 0 or sft == 0:
            return gf_ref[rows(start, rc)]
        return gs_ref[sft - 1, :, start - sft:start - sft + rc, :]

    cb = cb_ref[...].reshape(1, 1, conv_w)
    lg = lg_ref[...].reshape(1, 1, conv_w)
    lb = lb_ref[...].reshape(1, 1, conv_w)

    for r0 in range(0, tt, rc):
        pos = pos0 + ti * tt + r0 + lax.broadcasted_iota(jnp.int32, row_iota_shape, time_axis)
        for gi, win in enumerate(POOL_WINDOWS):
            lanes = slice(gi * group_w, (gi + 1) * group_w)
            u_new = uf_ref[rows(POOL_HALO + r0, rc, lanes)]
            s = u_new
            for i in range(1, win):
                s = s + uf_ref[rows(POOL_HALO + r0 - i, rc, lanes)]
            cnt = jnp.minimum(win, pos + 1).astype(F32)
            d_ref[rows(r0, rc, lanes)] = s / cnt - u_new

        base = conv_halo - n_conv_hist + r0
        acc = jnp.zeros(shape(rc, conv_w), F32)
        for k in range(taps):
            acc = acc + conv_rows(base + k) * cw_ref[k:k + 1, :].reshape(1, 1, conv_w)
        y = acc + cb
        mu = jnp.mean(y, axis=-1, keepdims=True)
        yc = y - mu
        var = jnp.mean(yc * yc, axis=-1, keepdims=True)
        yn = yc * lax.rsqrt(var + EPS) * lg + lb
        o_ref[rows(r0, rc, slice(pool_w, pool_w + conv_w))] = yn * _sigmoid(yn)

    for gi in range(len(POOL_WINDOWS)):
        lanes = slice(gi * group_w, (gi + 1) * group_w)
        d = d_ref[:, :, lanes].reshape(nb * tt, group_w).astype(BF16)
        y = _dot(d, wg_ref[gi]) * sc_ref[:, lanes]
        o_ref[:, :, lanes] = y.reshape(shape(tt, group_w))

    if n_t > 1:
        uf_ref[rows(0, POOL_HALO)] = uf_ref[rows(tt, POOL_HALO)]
        gf_ref[rows(0, conv_halo)] = gf_ref[rows(tt, conv_halo)]


def _mix_core(ug, hist_pool, hist_conv, w_grp, scale, conv_w, conv_b, ln_g, ln_b, *,
              time_axis, nb, tt, pos0):
    seq_axis = 1 - time_axis
    n_seq, t, width = ug.shape[seq_axis], ug.shape[time_axis], ug.shape[2]
    pool_w = hist_pool.shape[-1]
    cw = hist_conv.shape[-1]
    n_t = t // tt
    conv_halo = -(-hist_conv.shape[time_axis] // V7X_SUBLANES) * V7X_SUBLANES

    def block(n_rows, c):
        if time_axis == 0:
            return pl.BlockSpec((n_rows, nb, c), lambda b, i: (i if n_rows == tt else 0, b, 0))
        return pl.BlockSpec((nb, n_rows, c), lambda b, i: (b, i if n_rows == tt else 0, 0))

    def scratch(n_rows, c):
        return pltpu.VMEM((n_rows, nb, c) if time_axis == 0 else (nb, n_rows, c), F32)

    return pl.pallas_call(
        functools.partial(_mix_core_kernel, time_axis=time_axis, nb=nb, tt=tt, pos0=pos0, n_t=n_t,
                          conv_halo=conv_halo),
        grid=(n_seq // nb, n_t),
        in_specs=[
            block(tt, width),
            block(hist_pool.shape[time_axis], pool_w),
            block(hist_conv.shape[time_axis], cw),
            _resident(w_grp.shape),
            _resident(scale.shape),
            _resident(conv_w.shape),
            _resident(conv_b.shape),
            _resident(ln_g.shape),
            _resident(ln_b.shape),
        ],
        out_specs=block(tt, width),
        out_shape=jax.ShapeDtypeStruct(ug.shape, F32),
        scratch_shapes=[scratch(POOL_HALO + tt, pool_w), scratch(conv_halo + tt, cw), scratch(tt, pool_w)]
        + ([pltpu.VMEM((V7X_SUBLANES - 1, nb, conv_halo + tt, cw), F32)] if time_axis == 1 else []),
        compiler_params=_params(2),
        name="mix_core",
    )(ug, hist_pool, hist_conv, w_grp, scale, conv_w, conv_b, ln_g, ln_b)


def _sb_scores(z, mask):
    sign = jnp.uint32(0x80000000)
    neg_abs = lax.bitcast_convert_type(lax.bitcast_convert_type(z, jnp.uint32) | sign, F32)
    sp = jnp.maximum(z, 0.0) + jnp.log(1.0 + jnp.exp2(neg_abs)) * LOG2E
    log_beta = z - sp
    if mask is not None:
        sp = jnp.where(mask, sp, 0.0)
        log_beta = jnp.where(mask, log_beta, -jnp.inf)
    return sp.astype(BF16), log_beta, sp[:, :1]


def _sb_apply(scores, carry, w, first=0):
    sp, log_beta, sp_first = scores
    later = _dot(sp, w) + carry
    return later[:, first:first + 1] + sp_first, jnp.exp2(log_beta - later)


def _sb_prompt_kernel(bias_ref, q_ref, kt_ref, v_ref, w_ref, o_ref, sp_ref, lb_ref, carry_ref, acc_ref,
                      *, tq, kb, hd):
    qi = pl.program_id(2)
    width = V7X_LANES
    n_cb = q_ref.shape[-1] // width
    heads = width // hd
    lane = lax.broadcasted_iota(jnp.int32, (tq, width), 1)
    tri = lax.broadcasted_iota(jnp.int32, (kb, kb), 1) < lax.broadcasted_iota(jnp.int32, (kb, kb), 0)
    w = w_ref[...]
    in_head = [(lane >= hh * hd) & (lane < (hh + 1) * hd) for hh in range(heads)]
    one_lanes = jnp.where(lane < 2, 1.0, 0.0).astype(BF16)
    krow = lax.broadcasted_iota(jnp.int32, (width, kb), 0)
    groups = []
    for cb in range(n_cb):
        q = q_ref[:, cb * width:(cb + 1) * width].astype(F32)
        for hh in range(heads):
            qa = jnp.concatenate([jnp.where(in_head[hh], q, 0.0).astype(BF16), one_lanes], axis=1)
            b = jnp.full((width, kb), bias_ref[(pl.program_id(1) * n_cb + cb) * heads + hh], F32)
            b_hi = b.astype(BF16).astype(F32)
            rows = jnp.where(krow == 0, b_hi, jnp.where(krow == 1, b - b_hi, 0.0)).astype(BF16)
            groups.append((cb, qa, rows))

    def score(slot, blk, r0, r1, mask):
        start = pl.multiple_of(blk * kb, kb)
        for g, (cb, qa, bias_rows) in enumerate(groups):
            kt = kt_ref[cb * width:(cb + 1) * width, pl.ds(start, kb)]
            z = _dot(qa[r0:r1], jnp.concatenate([kt, bias_rows], axis=0))
            sp, log_beta, _ = _sb_scores(z, mask)
            sp_ref[slot, g, r0:r1] = sp
            lb_ref[slot, g, r0:r1] = log_beta

    def accumulate(slot, blk, r0, r1):
        start = pl.multiple_of(blk * kb, kb)
        for g, (cb, _, _) in enumerate(groups):
            sp = sp_ref[slot, g, r0:r1]
            scores = (sp, lb_ref[slot, g, r0:r1], sp[:, :1].astype(F32))
            carry_ref[g, r0:r1], a = _sb_apply(scores, carry_ref[g, r0:r1], w)
            acc_ref[g, r0:r1] += _dot(a.astype(BF16), v_ref[pl.ds(start, kb), cb * width:(cb + 1) * width])

    first = 2 * qi
    carry_ref[...] = jnp.zeros(carry_ref.shape, F32)
    acc_ref[...] = jnp.zeros(acc_ref.shape, F32)
    score(0, first + 1, kb, tq, tri)
    score(1, first, 0, kb, tri)
    score(1, first, kb, tq, None)
    accumulate(0, first + 1, kb, tq)

    @pl.loop(0, qi)
    def _(i):
        blk = first - 1 - 2 * i
        score(0, blk, 0, tq, None)
        accumulate(1, blk + 1, 0, tq)
        score(1, blk - 1, 0, tq, None)
        accumulate(0, blk, 0, tq)

    accumulate(1, 0, 0, tq)
    for cb in range(n_cb):
        out = acc_ref[cb * heads]
        for hh in range(1, heads):
            out = jnp.where(in_head[hh], acc_ref[cb * heads + hh], out)
        o_ref[:, cb * width:(cb + 1) * width] = out


def _sb_prompt(q, kt, v, bias, w, *, hd):
    b, s, d = q.shape
    kb = w.shape[1]
    tq = 2 * kb
    cols = ATTN_COL_BLOCKS * V7X_LANES
    n_groups = cols // hd
    qtile = pl.BlockSpec((None, tq, cols), lambda bi, p, i, *_: (bi, i, p))
    return pl.pallas_call(
        functools.partial(_sb_prompt_kernel, tq=tq, kb=kb, hd=hd),
        grid_spec=pltpu.PrefetchScalarGridSpec(
            num_scalar_prefetch=1,
            grid=(b, d // cols, s // tq),
            in_specs=[
                qtile,
                pl.BlockSpec((None, cols, s), lambda bi, p, i, *_: (bi, p, 0)),
                pl.BlockSpec((None, s, cols), lambda bi, p, i, *_: (bi, 0, p)),
                _resident(w.shape),
            ],
            out_specs=qtile,
            scratch_shapes=[
                pltpu.VMEM((2, n_groups, tq, kb), BF16),
                pltpu.VMEM((2, n_groups, tq, kb), F32),
                pltpu.VMEM((n_groups, tq, 1), F32),
                pltpu.VMEM((n_groups, tq, V7X_LANES), F32),
            ],
        ),
        out_shape=jax.ShapeDtypeStruct((b, s, d), F32),
        compiler_params=_params(3),
        name="sb_prompt",
    )(bias, q, kt, v, w)


def _sb_sample_kernel(pt_ref, q_ref, kn_ref, vn_ref, *refs, n_heads, hd, t_new, n_slots):
    kc_refs, vc_refs = refs[:n_slots], refs[n_slots:2 * n_slots]
    w_ref, wall_ref, b_ref, o_ref, qcb_ref, carry_ref, acc_ref = refs[2 * n_slots:]
    j = pl.program_id(1)
    rows = n_heads * t_new
    page = kc_refs[0].shape[-1]
    d = q_ref.shape[-1]
    width = V7X_LANES
    n_cb = d // width
    cb_heads = width // hd
    cb_rows = cb_heads * t_new
    bias = b_ref[...]
    lane = lax.broadcasted_iota(jnp.int32, (t_new, width), 1)

    def cols(cb):
        return slice(cb * width, (cb + 1) * width)

    def head_rows(cb):
        return slice(cb * cb_rows, (cb + 1) * cb_rows)

    @pl.when(j == 0)
    def _new_keys():
        for cb in range(n_cb):
            q = q_ref[:, cols(cb)]
            qcb_ref[cb] = jnp.concatenate(
                [jnp.where(lane // hd == hh, q, 0.0) for hh in range(cb_heads)], axis=0).astype(BF16)
        pad = jnp.zeros((page - t_new, d), F32)
        kn = jnp.concatenate([kn_ref[...], pad], axis=0).astype(BF16)
        vn = jnp.concatenate([vn_ref[...], pad], axis=0).astype(BF16)
        rr = lax.broadcasted_iota(jnp.int32, (rows, page), 0)
        cc = lax.broadcasted_iota(jnp.int32, (rows, page), 1)
        z = jnp.concatenate([_dot_nt(qcb_ref[cb], kn[:, cols(cb)]) for cb in range(n_cb)], axis=0) + bias
        carry, a = _sb_apply(_sb_scores(z, cc < rr % t_new), jnp.zeros((rows, 1), F32), w_ref[...])
        carry_ref[...] = carry
        a = a.astype(BF16)
        for cb in range(n_cb):
            acc_ref[cb] = _dot(a[head_rows(cb)], vn[:, cols(cb)])

    kt = jnp.concatenate([kc_ref[...].astype(BF16) for kc_ref in kc_refs], axis=1)
    vt = jnp.concatenate([vc_ref[...].astype(BF16) for vc_ref in vc_refs], axis=1)
    z = jnp.concatenate([_dot(qcb_ref[cb], kt[cols(cb)]) for cb in range(n_cb)], axis=0)
    sp, log_beta, _ = _sb_scores(z + jnp.concatenate([bias] * n_slots, axis=1), None)
    tile = 2 * page
    later = jnp.concatenate(
        [_dot(sp[:, :(c + 1) * tile], wall_ref[:(c + 1) * tile, c * tile:(c + 1) * tile])
         for c in range(n_slots * page // tile)], axis=1) + carry_ref[...]
    a = jnp.exp2(log_beta - later).astype(BF16)
    oldest = (n_slots - 1) * page
    carry_ref[...] = later[:, oldest:oldest + 1] + sp[:, oldest:oldest + 1].astype(F32)
    for cb in range(n_cb):
        acc_ref[cb] += _dot_nt(a[head_rows(cb)], vt[cols(cb)])

    @pl.when(j == pl.num_programs(1) - 1)
    def _emit():
        for cb in range(n_cb):
            acc = acc_ref[cb]
            out = acc[:t_new]
            for hh in range(1, cb_heads):
                out = jnp.where(lane // hd == hh, acc[hh * t_new:(hh + 1) * t_new], out)
            o_ref[:, cols(cb)] = out


def _sb_sample_post_kernel(pt_ref, *refs, n_slots, steps_per_tile, sample_kw, post_kw):
    n_sample_in = 3 + 2 * n_slots + 3
    sample_in, post_in = refs[:n_sample_in], refs[n_sample_in:n_sample_in + 7]
    o_sample_ref, o_post_ref = refs[n_sample_in + 7:n_sample_in + 9]
    scratch = refs[n_sample_in + 9:]
    _sb_sample_kernel(pt_ref, *sample_in, o_sample_ref, *scratch[:3], n_slots=n_slots, **sample_kw)

    h_ref, a_ref, wo_ref, g_ref, wup_ref, wdn_ref, gf_ref = post_in
    xn_ref, h1_ref = scratch[3:]
    step = pl.program_id(0) * pl.num_programs(1) + pl.program_id(1)
    phase = step % steps_per_tile
    ck = wup_ref.shape[1] // steps_per_tile

    @pl.when(phase == 0)
    def _project():
        h1 = h_ref[...] + _dot(a_ref[...].astype(BF16), wo_ref[...])
        h1_ref[...] = h1
        xn_ref[...] = _rmsnorm(h1, g_ref[...]).astype(BF16)

    for c in range(steps_per_tile):
        @pl.when(phase == c)
        def _mlp_chunk():
            up = _dot(xn_ref[...], wup_ref[:, c * ck:(c + 1) * ck])
            act = jnp.square(jnp.maximum(up, 0.0)).astype(BF16)
            h1_ref[...] += _dot(act, wdn_ref[c * ck:(c + 1) * ck, :])

    @pl.when(phase == steps_per_tile - 1)
    def _emit_tile():
        out = h1_ref[...]
        o_post_ref[...] = _rmsnorm(out, gf_ref[...]) if post_kw["final"] else out


def _sb_sample_post(page_table, q, k_new, v_new, cache_kt, cache_vt, layer, bias_rows, w, post_args, *,
                    n_heads, hd):
    h, a, wo, wo_layer, g, wup, wdn, mlp_layer, g_final, final = post_args
    db, t_new, d = q.shape
    n_pages = page_table.shape[1]
    page = cache_kt.shape[-1]
    rows = n_heads * t_new
    n_slots = min(SAMPLE_PAGES_PER_STEP, n_pages)
    n_j = n_pages // n_slots
    n_tok = h.shape[0]
    tm = min(FUSED_TOKEN_TILE, n_tok)
    steps_per_tile = db * n_j * tm // n_tok
    assert steps_per_tile >= 1 and steps_per_tile * n_tok == db * n_j * tm
    seq = pl.BlockSpec((None, t_new, d), lambda b, j, pt: (b, 0, 0))
    tok = lambda width: pl.BlockSpec((tm, width), lambda b, j, pt: ((b * n_j + j) // steps_per_tile, 0))

    def cache(slot):
        return pl.BlockSpec(
            (None, None, d, page),
            lambda b, j, pt: (layer, pt[b, n_pages - 1 - (j * n_slots + slot)], 0, 0))

    slots = [cache(u) for u in range(n_slots)]
    kj = lax.broadcasted_iota(jnp.int32, (n_slots * page, n_slots * page), 0)
    ks = lax.broadcasted_iota(jnp.int32, (n_slots * page, n_slots * page), 1)
    w_step = ((kj // page < ks // page) | ((kj // page == ks // page) & (kj > ks))).astype(BF16)
    return pl.pallas_call(
        functools.partial(
            _sb_sample_post_kernel, n_slots=n_slots, steps_per_tile=steps_per_tile,
            sample_kw=dict(n_heads=n_heads, hd=hd, t_new=t_new),
            post_kw=dict(final=final)),
        grid_spec=pltpu.PrefetchScalarGridSpec(
            num_scalar_prefetch=1,
            grid=(db, n_j),
            in_specs=[seq, seq, seq] + slots + slots + [
                _resident(w.shape), _resident(w_step.shape), _resident(bias_rows.shape),
                tok(d), tok(a.shape[1]), _layer_resident(wo, wo_layer), _resident((1, d)),
                _layer_resident(wup, mlp_layer), _layer_resident(wdn, mlp_layer), _resident((1, d)),
            ],
            out_specs=[seq, tok(d)],
            scratch_shapes=[
                pltpu.VMEM((d // V7X_LANES, rows * V7X_LANES // d, V7X_LANES), BF16),
                pltpu.VMEM((rows, 1), F32),
                pltpu.VMEM((d // V7X_LANES, rows * V7X_LANES // d, V7X_LANES), F32),
                pltpu.VMEM((tm, d), BF16),
                pltpu.VMEM((tm, d), F32),
            ],
        ),
        out_shape=[jax.ShapeDtypeStruct((db, t_new, d), F32), jax.ShapeDtypeStruct((n_tok, d), F32)],
        compiler_params=_params(2),
        name="sb_sample_post",
    )(page_table, q, k_new, v_new, *[cache_kt] * n_slots, *[cache_vt] * n_slots, w, w_step, bias_rows,
      h, a, wo, g, wup, wdn, g_final)


def _suffix_sum_matrix(kb):
    j = lax.broadcasted_iota(jnp.int32, (kb, kb), 0)
    s = lax.broadcasted_iota(jnp.int32, (kb, kb), 1)
    return (j > s).astype(BF16)


def kernel(x_prompt, x_sample, cache_k, cache_v, state_pool, state_conv, page_table, norm_mix, norm_mlp, norm_final, w_in_mix, w_out_mix, w_pool_grp, pool_scale, conv_w, conv_b, conv_ln_g, conv_ln_b, w_qkv, w_o, sb_bias, w_up, w_down):
    b, s, d = x_prompt.shape
    db, t_new, _ = x_sample.shape
    depth = norm_mix.shape[0]
    n_sb, n_pool_pages, page, n_heads, hd = cache_k.shape
    past_len = page_table.shape[1] * page
    pool_w = state_pool.shape[-1]
    conv_width = state_conv.shape[-1]
    n_pool_hist = state_pool.shape[2]
    n_conv_hist = state_conv.shape[2]
    q_scale = float(hd) ** -0.5 * LOG2E
    bias2 = sb_bias * LOG2E

    row = lambda x: x.reshape(1, -1)
    hp = x_prompt.reshape(b * s, d)
    hs = x_sample.transpose(1, 0, 2).reshape(t_new * db, d)
    cache_kt = cache_k.transpose(0, 1, 3, 4, 2).reshape(n_sb, n_pool_pages, d, page)
    cache_vt = cache_v.transpose(0, 1, 3, 4, 2).reshape(n_sb, n_pool_pages, d, page)
    state_pool_t = state_pool.transpose(0, 2, 1, 3)
    state_conv_t = state_conv.transpose(0, 2, 1, 3)
    w_sums_prompt = _suffix_sum_matrix(min(ATTN_K_BLOCK, s // 2))
    w_sums_sample = _suffix_sum_matrix(page)
    g_final = row(norm_final)
    w_up, w_down, w_in_mix, w_out_mix, w_o = (w.astype(BF16) for w in (w_up, w_down, w_in_mix, w_out_mix, w_o))

    kt_p = vt_p = kt_s = vt_s = None
    pool_p, pool_s, conv_p, conv_s = [], [], [], []
    for l in range(depth):
        final = l == depth - 1
        g_mix = row(norm_mix[l])
        mlp_w = (row(norm_mlp[l]), w_up, w_down, l, g_final, final)
        if l % 2 == 0:
            m = l // 2
            core_w = (w_pool_grp[m].astype(BF16), row(pool_scale[m]), conv_w[m], row(conv_b[m]),
                      row(conv_ln_g[m]), row(conv_ln_b[m]))

            ug_p = _mix_in(hp, g_mix, w_in_mix, m).reshape(b, s, pool_w + conv_width)
            ug_s = _mix_in(hs, g_mix, w_in_mix, m).reshape(t_new, db, pool_w + conv_width)
            cat_p = _mix_core(ug_p, jnp.zeros((b, n_pool_hist, pool_w), F32),
                              jnp.zeros((b, n_conv_hist, conv_width), F32), *core_w,
                              time_axis=1, nb=1, tt=min(TOKEN_TILE, s), pos0=0)
            cat_s = _mix_core(ug_s, state_pool_t[m], state_conv_t[m], *core_w,
                              time_axis=0, nb=min(SAMPLE_SEQS, db), tt=t_new, pos0=past_len)
            pool_p.append(ug_p[:, s - n_pool_hist:, :pool_w])
            conv_p.append(ug_p[:, s - n_conv_hist:, pool_w:])
            pool_s.append(jnp.concatenate([state_pool_t[m], ug_s[:, :, :pool_w]], axis=0)[-n_pool_hist:])
            conv_s.append(jnp.concatenate([state_conv_t[m], ug_s[:, :, pool_w:]], axis=0)[-n_conv_hist:])
            hp = _post(hp, cat_p.reshape(b * s, -1), w_out_mix, m, *mlp_w)
            hs = _post(hs, cat_s.reshape(t_new * db, -1), w_out_mix, m, *mlp_w)
        else:
            a = l // 2
            wq, wk, wv = (w_qkv[a][:, i * d:(i + 1) * d].astype(BF16) for i in range(3))
            qkv_w = (wq, wk, wv, wk.T, wv.T, q_scale)

            q, ktb, vb, kt_p, vt_p = _qkv(hp, g_mix, *qkv_w, kt_p, vt_p, a, n_sb, b, False)
            o_p = _sb_prompt(q.reshape(b, s, d), ktb, vb.reshape(b, s, d), bias2[a], w_sums_prompt, hd=hd)

            q, kb, vb, kt_s, vt_s = _qkv(hs, g_mix, *qkv_w, kt_s, vt_s, a, n_sb, t_new, True)
            by_seq = lambda x: x.reshape(t_new, db, d).transpose(1, 0, 2).astype(F32)
            bias_rows = jnp.broadcast_to(jnp.repeat(bias2[a], t_new)[:, None], (n_heads * t_new, page))
            o_s, hp = _sb_sample_post(page_table, by_seq(q), by_seq(kb), by_seq(vb), cache_kt, cache_vt, a,
                                      bias_rows, w_sums_sample, (hp, o_p.reshape(b * s, d), w_o, a, *mlp_w),
                                      n_heads=n_heads, hd=hd)
            hs = _post(hs, o_s.transpose(1, 0, 2).reshape(t_new * db, d), w_o, a, *mlp_w)

    kv_prompt = lambda x: x.reshape(n_sb, b, n_heads, hd, s).transpose(0, 1, 4, 2, 3)
    kv_sample = lambda x: x.reshape(n_sb, t_new, n_heads, hd, db).transpose(0, 4, 1, 2, 3)
    state_sample = lambda xs: jnp.stack(xs).transpose(0, 2, 1, 3)
    return (hp.reshape(b, s, d), hs.reshape(t_new, db, d).transpose(1, 0, 2),
            kv_prompt(kt_p), kv_prompt(vt_p), kv_sample(kt_s), kv_sample(vt_s),
            jnp.stack(pool_p), state_sample(pool_s), jnp.stack(conv_p), state_sample(conv_s))
```
